```python
import jax
import jax.numpy as jnp
from jax import lax
import numpy as np

D_MODEL = 1024
BATCH = 32
SEQ = 256
DEPTH = 2
DEC_BATCH = 4
DEC_SEQ = 4096
PAST_LEN = 256

GRID_W = 64
BLOCK = 128
WINDOW = 128
ATTN_HEADS = 8
KV_HEADS = 2
GROUP = ATTN_HEADS // KV_HEADS
HEAD_DIM = 64
D_ATTN = ATTN_HEADS * HEAD_DIM
D_ATTN_IN = D_ATTN + 2 * KV_HEADS * HEAD_DIM
RWKV_HEADS = 8
RWKV_HEAD_DIM = 64
D_RWKV = RWKV_HEADS * RWKV_HEAD_DIM
LORA_W = 64
LORA_A = 64
LORA_G = 128
D_RWKV_IN = 3 * D_RWKV + 2 * LORA_W + 2 * LORA_A + LORA_G
D_IN_AB = D_ATTN_IN + D_RWKV_IN
SHIFT_WIDTH = 3
ROPE_BASE = 10000.0
D_CONV = D_MODEL
CONV_WIDTH = 31
PEER_HEADS = 8
N_KEYS = 128
N_EXPERTS = N_KEYS * N_KEYS
D_QUERY = 256
PEER_TOPK = 16
N_EVEN = (DEPTH + 1) // 2
N_ODD = DEPTH // 2
RMS_EPS = 1e-6
LN_EPS = 1e-5
GN_EPS = 64e-5

kernel_name = 'hybrid_diffusion_prefix_trunk_step'


def rmsnorm(x, g):
    xf = x.astype(jnp.float32)
    y = xf * lax.rsqrt(jnp.mean(xf * xf, axis=-1, keepdims=True) + RMS_EPS)
    return (y * g.astype(jnp.float32)).astype(x.dtype)


def layernorm(x, g, b, eps):
    xf = x.astype(jnp.float32)
    mu = jnp.mean(xf, axis=-1, keepdims=True)
    var = jnp.mean(jnp.square(xf - mu), axis=-1, keepdims=True)
    y = (xf - mu) * lax.rsqrt(var + eps)
    return (y * g.astype(jnp.float32) + b.astype(jnp.float32)).astype(x.dtype)


def modulation(cond, w, b):
    m = jax.nn.silu(cond) @ w + b
    return jnp.split(m[:, None, :], 6, axis=-1)


def modulate(h, shift, scale):
    return h * (1 + scale) + shift


def depthwise_conv(x, w):
    return lax.conv_general_dilated(
        x, w[:, None, :].astype(x.dtype), window_strides=(1,), padding='SAME',
        dimension_numbers=('NWC', 'WIO', 'NWC'), feature_group_count=x.shape[-1])


def axial_rope(x):
    t = x.shape[1]
    rows = t // GRID_W
    row_id = jnp.repeat(jnp.arange(rows, dtype=jnp.float32), GRID_W)
    col_id = jnp.tile(jnp.arange(GRID_W, dtype=jnp.float32), rows)
    half = HEAD_DIM // 2
    quarter = half // 2
    freqs = ROPE_BASE ** (-jnp.arange(quarter, dtype=jnp.float32) / quarter)

    def rotate(xa, pos):
        ang = pos[:, None] * freqs[None, :]
        cos = jnp.cos(ang)[None, :, None, :]
        sin = jnp.sin(ang)[None, :, None, :]
        xa = xa.astype(jnp.float32)
        a, b = xa[..., :quarter], xa[..., quarter:]
        return jnp.concatenate([a * cos - b * sin, b * cos + a * sin], axis=-1)

    y = jnp.concatenate([rotate(x[..., :half], row_id), rotate(x[..., half:], col_id)], axis=-1)
    return y.astype(x.dtype)


def context_attention(q, k, v, sink):
    b, c = q.shape[:2]
    nqb = c // BLOCK
    scale = HEAD_DIM ** -0.5
    qb = q.reshape(b, nqb, BLOCK, KV_HEADS, GROUP, HEAD_DIM).transpose(1, 0, 2, 3, 4, 5)
    sink_l = jnp.broadcast_to(sink.astype(jnp.float32).reshape(1, KV_HEADS, GROUP, 1, 1),
                              (b, KV_HEADS, GROUP, BLOCK, 1))

    def one(qblk):
        s = jnp.einsum('bqkgd,bckd->bkgqc', qblk, k).astype(jnp.float32) * scale
        p = jax.nn.softmax(jnp.concatenate([s, sink_l], axis=-1), axis=-1)[..., :c]
        return jnp.einsum('bkgqc,bckd->bqkgd', p.astype(v.dtype), v)

    out = lax.map(one, qb)
    return out.transpose(1, 0, 2, 3, 4, 5).reshape(b, c, D_ATTN)


def window_attention(q, k, v, k_ctx, v_ctx, sink):
    b, t = q.shape[:2]
    c = k_ctx.shape[1]
    nb = t // BLOCK
    scale = HEAD_DIM ** -0.5
    qb = q.reshape(b, nb, BLOCK, KV_HEADS, GROUP, HEAD_DIM)
    pad = ((0, 0), (BLOCK, BLOCK), (0, 0), (0, 0))
    kp = jnp.pad(k, pad).reshape(b, nb + 2, BLOCK, KV_HEADS, HEAD_DIM)
    vp = jnp.pad(v, pad).reshape(b, nb + 2, BLOCK, KV_HEADS, HEAD_DIM)
    kb = jnp.concatenate([kp[:, :-2], kp[:, 1:-1], kp[:, 2:]], axis=2)
    vb = jnp.concatenate([vp[:, :-2], vp[:, 1:-1], vp[:, 2:]], axis=2)
    qpos = jnp.arange(t).reshape(nb, BLOCK)
    kpos = (jnp.arange(nb)[:, None] - 1) * BLOCK + jnp.arange(3 * BLOCK)[None, :]
    rel = kpos[:, None, :] - qpos[:, :, None]
    valid = (jnp.abs(rel) <= WINDOW) & (kpos[:, None, :] >= 0) & (kpos[:, None, :] < t)
    s_loc = jnp.einsum('bnqkgd,bnskd->bnkgqs', qb, kb).astype(jnp.float32) * scale
    s_loc = jnp.where(valid[None, :, None, None], s_loc, -jnp.inf)
    s_ctx = jnp.einsum('bnqkgd,bckd->bnkgqc', qb, k_ctx).astype(jnp.float32) * scale
    sink_l = jnp.broadcast_to(sink.astype(jnp.float32).reshape(1, 1, KV_HEADS, GROUP, 1, 1),
                              s_loc.shape[:-1] + (1,))
    p = jax.nn.softmax(jnp.concatenate([s_loc, s_ctx, sink_l], axis=-1), axis=-1)
    p_loc = p[..., :3 * BLOCK].astype(v.dtype)
    p_ctx = p[..., 3 * BLOCK:3 * BLOCK + c].astype(v.dtype)
    o = (jnp.einsum('bnkgqs,bnskd->bnqkgd', p_loc, vb)
         + jnp.einsum('bnkgqc,bckd->bnqkgd', p_ctx, v_ctx))
    return o.reshape(b, t, D_ATTN)


def rwkv_scan(r, w, kh, ka, kt, v, s0, reverse):
    def step(s, inp):
        r_t, w_t, kh_t, ka_t, kt_t, v_t = inp
        s = (s * w_t[:, :, None, :]
             - jnp.einsum('bhvk,bhk->bhv', s, kh_t)[..., None] * ka_t[:, :, None, :]
             + v_t[..., None] * kt_t[:, :, None, :])
        return s, jnp.einsum('bhvk,bhk->bhv', s, r_t)

    xs = tuple(jnp.moveaxis(a, 1, 0) for a in (r, w, kh, ka, kt, v))
    s_fin, y = lax.scan(step, s0, xs, reverse=reverse)
    return jnp.moveaxis(y, 0, 1), s_fin


def rwkv_mix(u, k_k, k_a, r_k, w0, w2, a0, a2, g2, gn_g, gn_b, s0):
    f32 = jnp.float32
    b, t = u.shape[:2]
    u = u.astype(f32)
    cuts = [D_RWKV, 2 * D_RWKV, 3 * D_RWKV, 3 * D_RWKV + LORA_W, 3 * D_RWKV + 2 * LORA_W,
            3 * D_RWKV + 2 * LORA_W + LORA_A, 3 * D_RWKV + 2 * LORA_W + 2 * LORA_A]
    r, k, v, wl_f, wl_b, al_f, al_b, gl = jnp.split(u, cuts, axis=-1)

    def heads(z):
        return z.reshape(b, t, RWKV_HEADS, RWKV_HEAD_DIM)

    kk = heads(k * k_k)
    kh = kk / jnp.maximum(jnp.linalg.norm(kk, axis=-1, keepdims=True), 1e-12)
    s0 = s0.astype(f32)

    def direction(idx, wl, al, reverse):
        logw = -jax.nn.softplus(-(w0[idx] + jnp.tanh(wl) @ w2[idx])) - 0.5
        decay = jnp.exp(-jnp.exp(logw))
        a = jax.nn.sigmoid(a0[idx] + al @ a2[idx])
        kt = heads(k * (1 + (a - 1) * k_a))
        y, s = rwkv_scan(heads(r), heads(decay), kh, kh * heads(a), kt, heads(v), s0[:, idx], reverse)
        return y, s, kt

    y_f, s_f, kt_f = direction(0, wl_f, al_f, False)
    y_b, s_b, kt_b = direction(1, wl_b, al_b, True)
    y = layernorm(y_f + y_b, gn_g.reshape(RWKV_HEADS, RWKV_HEAD_DIM),
                  gn_b.reshape(RWKV_HEADS, RWKV_HEAD_DIM), GN_EPS)
    bonus = jnp.sum(heads(r) * 0.5 * (kt_f + kt_b) * r_k.reshape(RWKV_HEADS, RWKV_HEAD_DIM),
                    axis=-1, keepdims=True) * heads(v)
    g = jax.nn.sigmoid(gl) @ g2
    out = (y + bonus).reshape(b, t, D_RWKV) * g
    return out, jnp.stack([s_f, s_b], axis=1)


def mixer_even(h, w_in, w_out, sink, shift_w, k_k, k_a, r_k, w0, w2, a0, a2, g2, gn_g, gn_b, ctx):
    b, t = h.shape[:2]
    d_kv = KV_HEADS * HEAD_DIM
    u = h @ w_in
    q = u[..., :D_ATTN].reshape(b, t, ATTN_HEADS, HEAD_DIM)
    k = u[..., D_ATTN:D_ATTN + d_kv].reshape(b, t, KV_HEADS, HEAD_DIM)
    v = u[..., D_ATTN + d_kv:D_ATTN_IN].reshape(b, t, KV_HEADS, HEAD_DIM)
    u_r = depthwise_conv(u[..., D_ATTN_IN:], shift_w)
    if ctx is None:
        a_out = context_attention(q, k, v, sink)
        s0 = jnp.zeros((b, 2, RWKV_HEADS, RWKV_HEAD_DIM, RWKV_HEAD_DIM), jnp.float32)
    else:
        k_ctx, v_ctx, s0 = ctx
        a_out = window_attention(axial_rope(q), axial_rope(k), v,
                                 k_ctx.astype(h.dtype), v_ctx.astype(h.dtype), sink)
    b_out, s_fin = rwkv_mix(u_r, k_k, k_a, r_k, w0, w2, a0, a2, g2, gn_g, gn_b, s0)
    y = jnp.concatenate([a_out, b_out.astype(a_out.dtype)], axis=-1) @ w_out
    return y, k, v, s_fin


def mixer_odd(h, w_pw1, b_pw1, w_dw, b_dw, ln_g, ln_b, w_pw2, b_pw2):
    u = h @ w_pw1 + b_pw1
    u = u[..., :D_CONV] * jax.nn.sigmoid(u[..., D_CONV:])
    u = depthwise_conv(u, w_dw) + b_dw
    u = jax.nn.silu(layernorm(u, ln_g, ln_b, LN_EPS))
    return u @ w_pw2 + b_pw2


def peer(h, w_q, keys, u_tab, v_tab):
    b, t, d = h.shape
    x = h.reshape(-1, BLOCK, d)
    half = D_QUERY // 2

    def one(xb):
        q = (xb @ w_q).reshape(BLOCK, PEER_HEADS, D_QUERY)
        s1 = jnp.einsum('thd,hnd->thn', q[..., :half], keys[0]).astype(jnp.float32)
        s2 = jnp.einsum('thd,hnd->thn', q[..., half:], keys[1]).astype(jnp.float32)
        v1, i1 = lax.top_k(s1, PEER_TOPK)
        v2, i2 = lax.top_k(s2, PEER_TOPK)
        cand = (v1[..., :, None] + v2[..., None, :]).reshape(BLOCK, PEER_HEADS, PEER_TOPK * PEER_TOPK)
        cidx = (i1[..., :, None] * N_KEYS + i2[..., None, :]).reshape(BLOCK, PEER_HEADS, PEER_TOPK * PEER_TOPK)
        sv, pos = lax.top_k(cand, PEER_TOPK)
        eidx = jnp.take_along_axis(cidx, pos, axis=-1)
        g = jax.nn.softmax(sv, axis=-1)
        act = jax.nn.gelu(jnp.einsum('thkd,td->thk', u_tab[eidx], xb), approximate=False)
        wgt = (g * act.astype(jnp.float32)).astype(xb.dtype)
        return jnp.einsum('thk,thkd->td', wgt, v_tab[eidx])

    return lax.map(one, x).reshape(b, t, d)


def setup_inputs(seed: int = 0) -> dict:
    key = jax.random.key(seed)
    keys = iter(jax.random.split(key, 64))

    def nrm(shape, scale):
        return jax.random.normal(next(keys), shape, jnp.float32) * scale

    d = D_MODEL
    return {
        'x_prompt': nrm((BATCH, SEQ, d), 1.0),
        'x_sample': nrm((DEC_BATCH, DEC_SEQ, d), 1.0),
        'cache_k': nrm((DEC_BATCH, N_EVEN, PAST_LEN, KV_HEADS, HEAD_DIM), 1.0),
        'cache_v': nrm((DEC_BATCH, N_EVEN, PAST_LEN, KV_HEADS, HEAD_DIM), 1.0),
        'state_rwkv': nrm((DEC_BATCH, N_EVEN, 2, RWKV_HEADS, RWKV_HEAD_DIM, RWKV_HEAD_DIM), 0.3),
        'c': nrm((DEC_BATCH, d), 1.0),
        'c_ctx': nrm((d,), 1.0),
        'mod_w': nrm((DEPTH, d, 6 * d), 0.5 * d ** -0.5),
        'mod_b': nrm((DEPTH, 6 * d), 0.02),
        'norm_g': 1.0 + nrm((DEPTH, 2, d), 0.05),
        'ab_w_in': nrm((N_EVEN, d, D_IN_AB), d ** -0.5),
        'ab_w_out': nrm((N_EVEN, D_ATTN + D_RWKV, d), (D_ATTN + D_RWKV) ** -0.5),
        'ab_sink': nrm((N_EVEN, ATTN_HEADS), 0.5),
        'ab_shift_w': nrm((N_EVEN, SHIFT_WIDTH, D_RWKV_IN), 0.1).at[:, SHIFT_WIDTH // 2].add(1.0),
        'ab_k_k': 0.85 + nrm((N_EVEN, D_RWKV), 0.05),
        'ab_k_a': 1.0 + nrm((N_EVEN, D_RWKV), 0.05),
        'ab_r_k': nrm((N_EVEN, D_RWKV), 0.1),
        'ab_w0': nrm((N_EVEN, 2, D_RWKV), 0.5),
        'ab_w2': nrm((N_EVEN, 2, LORA_W, D_RWKV), 0.1),
        'ab_a0': nrm((N_EVEN, 2, D_RWKV), 0.1),
        'ab_a2': nrm((N_EVEN, 2, LORA_A, D_RWKV), 0.1),
        'ab_g2': nrm((N_EVEN, LORA_G, D_RWKV), LORA_G ** -0.5),
        'ab_gn_g': 1.0 + nrm((N_EVEN, D_RWKV), 0.05),
        'ab_gn_b': nrm((N_EVEN, D_RWKV), 0.02),
        'cv_w_pw1': nrm((N_ODD, d, 2 * D_CONV), d ** -0.5),
        'cv_b_pw1': nrm((N_ODD, 2 * D_CONV), 0.02),
        'cv_w_dw': nrm((N_ODD, CONV_WIDTH, D_CONV), CONV_WIDTH ** -0.5),
        'cv_b_dw': nrm((N_ODD, D_CONV), 0.02),
        'cv_ln_g': 1.0 + nrm((N_ODD, D_CONV), 0.05),
        'cv_ln_b': nrm((N_ODD, D_CONV), 0.02),
        'cv_w_pw2': nrm((N_ODD, D_CONV, d), D_CONV ** -0.5),
        'cv_b_pw2': nrm((N_ODD, d), 0.02),
        'peer_w_q': nrm((DEPTH, d, PEER_HEADS * D_QUERY), d ** -0.5),
        'peer_keys': nrm((DEPTH, 2, PEER_HEADS, N_KEYS, D_QUERY // 2), (D_QUERY // 2) ** -0.5),
        'peer_u': nrm((DEPTH, N_EXPERTS, d), d ** -0.5),
        'peer_v': nrm((DEPTH, N_EXPERTS, d), 0.5),
        'final_g': 1.0 + nrm((d,), 0.05),
    }


def reference(x_prompt, x_sample, cache_k, cache_v, state_rwkv, c, c_ctx, mod_w, mod_b, norm_g,
              ab_w_in, ab_w_out, ab_sink, ab_shift_w, ab_k_k, ab_k_a, ab_r_k, ab_w0, ab_w2,
              ab_a0, ab_a2, ab_g2, ab_gn_g, ab_gn_b, cv_w_pw1, cv_b_pw1, cv_w_dw, cv_b_dw,
              cv_ln_g, cv_ln_b, cv_w_pw2, cv_b_pw2, peer_w_q, peer_keys, peer_u, peer_v, final_g):
    xp = x_prompt
    xs = x_sample
    new_k, new_v, new_s = [], [], []
    for layer in range(DEPTH):
        i = layer // 2
        sp1, cp1, gp1, sp2, cp2, gp2 = modulation(c_ctx[None, :], mod_w[layer], mod_b[layer])
        ss1, cs1, gs1, ss2, cs2, gs2 = modulation(c, mod_w[layer], mod_b[layer])
        hp = modulate(rmsnorm(xp, norm_g[layer, 0]), sp1, cp1)
        hs = modulate(rmsnorm(xs, norm_g[layer, 0]), ss1, cs1)
        if layer % 2 == 0:
            prm = (ab_w_in[i], ab_w_out[i], ab_sink[i], ab_shift_w[i], ab_k_k[i], ab_k_a[i],
                   ab_r_k[i], ab_w0[i], ab_w2[i], ab_a0[i], ab_a2[i], ab_g2[i], ab_gn_g[i], ab_gn_b[i])
            yp, k_c, v_c, s_c = mixer_even(hp, *prm, None)
            ys, _, _, _ = mixer_even(hs, *prm, (cache_k[:, i], cache_v[:, i], state_rwkv[:, i]))
            new_k.append(k_c)
            new_v.append(v_c)
            new_s.append(s_c)
        else:
            prm = (cv_w_pw1[i], cv_b_pw1[i], cv_w_dw[i], cv_b_dw[i], cv_ln_g[i], cv_ln_b[i],
                   cv_w_pw2[i], cv_b_pw2[i])
            yp = mixer_odd(hp, *prm)
            ys = mixer_odd(hs, *prm)
        xp = xp + gp1 * yp
        xs = xs + gs1 * ys
        hp = modulate(rmsnorm(xp, norm_g[layer, 1]), sp2, cp2)
        hs = modulate(rmsnorm(xs, norm_g[layer, 1]), ss2, cs2)
        pprm = (peer_w_q[layer], peer_keys[layer], peer_u[layer], peer_v[layer])
        xp = xp + gp2 * peer(hp, *pprm)
        xs = xs + gs2 * peer(hs, *pprm)
    y_prompt = rmsnorm(xp, final_g)
    y_sample = rmsnorm(xs, final_g)
    new_cache_k = jnp.stack(new_k, axis=1)
    new_cache_v = jnp.stack(new_v, axis=1)
    new_state_rwkv = jnp.stack(new_s, axis=1)
    return (y_prompt, y_sample, new_cache_k, new_cache_v, new_state_rwkv)
```

```python
import functools
import jax
import jax.numpy as jnp
from jax import lax
import numpy as np
from jax.experimental import pallas as pl
from jax.experimental.pallas import tpu as pltpu

D_MODEL = 1024
BATCH = 32
SEQ = 256
DEPTH = 2
DEC_BATCH = 4
DEC_SEQ = 4096
PAST_LEN = 256

GRID_W = 64
BLOCK = 128
WINDOW = 128
ATTN_HEADS = 8
KV_HEADS = 2
GROUP = ATTN_HEADS // KV_HEADS
HEAD_DIM = 64
D_ATTN = ATTN_HEADS * HEAD_DIM
D_ATTN_IN = D_ATTN + 2 * KV_HEADS * HEAD_DIM
RWKV_HEADS = 8
RWKV_HEAD_DIM = 64
D_RWKV = RWKV_HEADS * RWKV_HEAD_DIM
LORA_W = 64
LORA_A = 64
LORA_G = 128
D_RWKV_IN = 3 * D_RWKV + 2 * LORA_W + 2 * LORA_A + LORA_G
D_IN_AB = D_ATTN_IN + D_RWKV_IN
SHIFT_WIDTH = 3
ROPE_BASE = 10000.0
D_CONV = D_MODEL
CONV_WIDTH = 31
PEER_HEADS = 8
N_KEYS = 128
N_EXPERTS = N_KEYS * N_KEYS
D_QUERY = 256
PEER_TOPK = 16
N_EVEN = (DEPTH + 1) // 2
N_ODD = DEPTH // 2
RMS_EPS = 1e-6
LN_EPS = 1e-5
GN_EPS = 64e-5
N_PROMPT = BATCH * SEQ
SAMPLE_LEN = DEC_SEQ
N_TOKENS = N_PROMPT + DEC_BATCH * DEC_SEQ
VMEM_LIMIT = 56 * 1024 * 1024


def rmsnorm(x, g):
    xf = x.astype(jnp.float32)
    y = xf * lax.rsqrt(jnp.mean(xf * xf, axis=-1, keepdims=True) + RMS_EPS)
    return (y * g.astype(jnp.float32)).astype(x.dtype)


def layernorm(x, g, b, eps):
    xf = x.astype(jnp.float32)
    mu = jnp.mean(xf, axis=-1, keepdims=True)
    var = jnp.mean(jnp.square(xf - mu), axis=-1, keepdims=True)
    y = (xf - mu) * lax.rsqrt(var + eps)
    return (y * g.astype(jnp.float32) + b.astype(jnp.float32)).astype(x.dtype)


def modulation(cond, w, b):
    m = jax.nn.silu(cond) @ w + b
    return jnp.split(m[:, None, :], 6, axis=-1)


def modulate(h, shift, scale):
    return h * (1 + scale) + shift


def depthwise_conv(x, w):
    return lax.conv_general_dilated(
        x, w[:, None, :].astype(x.dtype), window_strides=(1,), padding='SAME',
        dimension_numbers=('NWC', 'WIO', 'NWC'), feature_group_count=x.shape[-1])


def axial_rope(x):
    t = x.shape[1]
    rows = t // GRID_W
    row_id = jnp.repeat(jnp.arange(rows, dtype=jnp.float32), GRID_W)
    col_id = jnp.tile(jnp.arange(GRID_W, dtype=jnp.float32), rows)
    half = HEAD_DIM // 2
    quarter = half // 2
    freqs = ROPE_BASE ** (-jnp.arange(quarter, dtype=jnp.float32) / quarter)

    def rotate(xa, pos):
        ang = pos[:, None] * freqs[None, :]
        cos = jnp.cos(ang)[None, :, None, :]
        sin = jnp.sin(ang)[None, :, None, :]
        xa = xa.astype(jnp.float32)
        a, b = xa[..., :quarter], xa[..., quarter:]
        return jnp.concatenate([a * cos - b * sin, b * cos + a * sin], axis=-1)

    y = jnp.concatenate([rotate(x[..., :half], row_id), rotate(x[..., half:], col_id)], axis=-1)
    return y.astype(x.dtype)


def context_attention(q, k, v, sink):
    b, c = q.shape[:2]
    nqb = c // BLOCK
    scale = HEAD_DIM ** -0.5
    qb = q.reshape(b, nqb, BLOCK, KV_HEADS, GROUP, HEAD_DIM).transpose(1, 0, 2, 3, 4, 5)
    sink_l = jnp.broadcast_to(sink.astype(jnp.float32).reshape(1, KV_HEADS, GROUP, 1, 1),
                              (b, KV_HEADS, GROUP, BLOCK, 1))

    def one(qblk):
        s = jnp.einsum('bqkgd,bckd->bkgqc', qblk, k).astype(jnp.float32) * scale
        p = jax.nn.softmax(jnp.concatenate([s, sink_l], axis=-1), axis=-1)[..., :c]
        return jnp.einsum('bkgqc,bckd->bqkgd', p.astype(v.dtype), v)

    out = lax.map(one, qb)
    return out.transpose(1, 0, 2, 3, 4, 5).reshape(b, c, D_ATTN)


def window_attention(q, k, v, k_ctx, v_ctx, sink):
    b, t = q.shape[:2]
    c = k_ctx.shape[1]
    nb = t // BLOCK
    scale = HEAD_DIM ** -0.5
    qb = q.reshape(b, nb, BLOCK, KV_HEADS, GROUP, HEAD_DIM)
    pad = ((0, 0), (BLOCK, BLOCK), (0, 0), (0, 0))
    kp = jnp.pad(k, pad).reshape(b, nb + 2, BLOCK, KV_HEADS, HEAD_DIM)
    vp = jnp.pad(v, pad).reshape(b, nb + 2, BLOCK, KV_HEADS, HEAD_DIM)
    kb = jnp.concatenate([kp[:, :-2], kp[:, 1:-1], kp[:, 2:]], axis=2)
    vb = jnp.concatenate([vp[:, :-2], vp[:, 1:-1], vp[:, 2:]], axis=2)
    qpos = jnp.arange(t).reshape(nb, BLOCK)
    kpos = (jnp.arange(nb)[:, None] - 1) * BLOCK + jnp.arange(3 * BLOCK)[None, :]
    rel = kpos[:, None, :] - qpos[:, :, None]
    valid = (jnp.abs(rel) <= WINDOW) & (kpos[:, None, :] >= 0) & (kpos[:, None, :] < t)
    s_loc = jnp.einsum('bnqkgd,bnskd->bnkgqs', qb, kb).astype(jnp.float32) * scale
    s_loc = jnp.where(valid[None, :, None, None], s_loc, -jnp.inf)
    s_ctx = jnp.einsum('bnqkgd,bckd->bnkgqc', qb, k_ctx).astype(jnp.float32) * scale
    sink_l = jnp.broadcast_to(sink.astype(jnp.float32).reshape(1, 1, KV_HEADS, GROUP, 1, 1),
                              s_loc.shape[:-1] + (1,))
    p = jax.nn.softmax(jnp.concatenate([s_loc, s_ctx, sink_l], axis=-1), axis=-1)
    p_loc = p[..., :3 * BLOCK].astype(v.dtype)
    p_ctx = p[..., 3 * BLOCK:3 * BLOCK + c].astype(v.dtype)
    o = (jnp.einsum('bnkgqs,bnskd->bnqkgd', p_loc, vb)
         + jnp.einsum('bnkgqc,bckd->bnqkgd', p_ctx, v_ctx))
    return o.reshape(b, t, D_ATTN)


def rwkv_scan(r, w, kh, ka, kt, v, s0, reverse):
    def step(s, inp):
        r_t, w_t, kh_t, ka_t, kt_t, v_t = inp
        s = (s * w_t[:, :, None, :]
             - jnp.einsum('bhvk,bhk->bhv', s, kh_t)[..., None] * ka_t[:, :, None, :]
             + v_t[..., None] * kt_t[:, :, None, :])
        return s, jnp.einsum('bhvk,bhk->bhv', s, r_t)

    xs = tuple(jnp.moveaxis(a, 1, 0) for a in (r, w, kh, ka, kt, v))
    s_fin, y = lax.scan(step, s0, xs, reverse=reverse)
    return jnp.moveaxis(y, 0, 1), s_fin


def rwkv_mix(u, k_k, k_a, r_k, w0, w2, a0, a2, g2, gn_g, gn_b, s0):
    f32 = jnp.float32
    b, t = u.shape[:2]
    u = u.astype(f32)
    cuts = [D_RWKV, 2 * D_RWKV, 3 * D_RWKV, 3 * D_RWKV + LORA_W, 3 * D_RWKV + 2 * LORA_W,
            3 * D_RWKV + 2 * LORA_W + LORA_A, 3 * D_RWKV + 2 * LORA_W + 2 * LORA_A]
    r, k, v, wl_f, wl_b, al_f, al_b, gl = jnp.split(u, cuts, axis=-1)

    def heads(z):
        return z.reshape(b, t, RWKV_HEADS, RWKV_HEAD_DIM)

    kk = heads(k * k_k)
    kh = kk / jnp.maximum(jnp.linalg.norm(kk, axis=-1, keepdims=True), 1e-12)
    s0 = s0.astype(f32)

    def direction(idx, wl, al, reverse):
        logw = -jax.nn.softplus(-(w0[idx] + jnp.tanh(wl) @ w2[idx])) - 0.5
        decay = jnp.exp(-jnp.exp(logw))
        a = jax.nn.sigmoid(a0[idx] + al @ a2[idx])
        kt = heads(k * (1 + (a - 1) * k_a))
        y, s = rwkv_scan(heads(r), heads(decay), kh, kh * heads(a), kt, heads(v), s0[:, idx], reverse)
        return y, s, kt

    y_f, s_f, kt_f = direction(0, wl_f, al_f, False)
    y_b, s_b, kt_b = direction(1, wl_b, al_b, True)
    y = layernorm(y_f + y_b, gn_g.reshape(RWKV_HEADS, RWKV_HEAD_DIM),
                  gn_b.reshape(RWKV_HEADS, RWKV_HEAD_DIM), GN_EPS)
    bonus = jnp.sum(heads(r) * 0.5 * (kt_f + kt_b) * r_k.reshape(RWKV_HEADS, RWKV_HEAD_DIM),
                    axis=-1, keepdims=True) * heads(v)
    g = jax.nn.sigmoid(gl) @ g2
    out = (y + bonus).reshape(b, t, D_RWKV) * g
    return out, jnp.stack([s_f, s_b], axis=1)


def mixer_even(h, w_in, w_out, sink, shift_w, k_k, k_a, r_k, w0, w2, a0, a2, g2, gn_g, gn_b, ctx):
    b, t = h.shape[:2]
    d_kv = KV_HEADS * HEAD_DIM
    u = h @ w_in
    q = u[..., :D_ATTN].reshape(b, t, ATTN_HEADS, HEAD_DIM)
    k = u[..., D_ATTN:D_ATTN + d_kv].reshape(b, t, KV_HEADS, HEAD_DIM)
    v = u[..., D_ATTN + d_kv:D_ATTN_IN].reshape(b, t, KV_HEADS, HEAD_DIM)
    u_r = depthwise_conv(u[..., D_ATTN_IN:], shift_w)
    if ctx is None:
        a_out = context_attention(q, k, v, sink)
        s0 = jnp.zeros((b, 2, RWKV_HEADS, RWKV_HEAD_DIM, RWKV_HEAD_DIM), jnp.float32)
    else:
        k_ctx, v_ctx, s0 = ctx
        a_out = window_attention(axial_rope(q), axial_rope(k), v,
                                 k_ctx.astype(h.dtype), v_ctx.astype(h.dtype), sink)
    b_out, s_fin = rwkv_mix(u_r, k_k, k_a, r_k, w0, w2, a0, a2, g2, gn_g, gn_b, s0)
    y = jnp.concatenate([a_out, b_out.astype(a_out.dtype)], axis=-1) @ w_out
    return y, k, v, s_fin


def mixer_odd(h, w_pw1, b_pw1, w_dw, b_dw, ln_g, ln_b, w_pw2, b_pw2):
    u = h @ w_pw1 + b_pw1
    u = u[..., :D_CONV] * jax.nn.sigmoid(u[..., D_CONV:])
    u = depthwise_conv(u, w_dw) + b_dw
    u = jax.nn.silu(layernorm(u, ln_g, ln_b, LN_EPS))
    return u @ w_pw2 + b_pw2


def _mod_row(i, rows_per_block):
    tok = i * rows_per_block
    return jnp.where(tok < N_PROMPT, 0, 1 + (tok - N_PROMPT) // SAMPLE_LEN)


def _norm_mod(x, g, shift, scale):
    y = x * lax.rsqrt(jnp.mean(x * x, axis=-1, keepdims=True) + RMS_EPS)
    return (y * g) * (1 + scale) + shift


def _top_values(s, n):
    vals = []
    cur = s
    for a in range(n):
        m = jnp.max(cur, axis=0, keepdims=True)
        vals.append(m)
        if a + 1 < n:
            cur = jnp.where(cur == m, -jnp.inf, cur)
    return vals


def _peer_route_body(x_ref, g_ref, shift_ref, scale_ref, wq_ref, k1_ref, k2_ref,
                     hb_ref, th_ref, e1_ref, s2_ref, e2_ref, hb_s):
    h = pl.program_id(1)

    @pl.when(h == 0)
    def _():
        hn = _norm_mod(x_ref[...], g_ref[...], shift_ref[0], scale_ref[0])
        hb_s[...] = hn.astype(jnp.bfloat16)
        hb_ref[...] = hn.astype(jnp.bfloat16)

    half = D_QUERY // 2
    q = jnp.dot(hb_s[...], wq_ref[...], preferred_element_type=jnp.float32).astype(jnp.bfloat16)
    nt = (((1,), (1,)), ((), ()))
    s1 = lax.dot_general(k1_ref[0], q[:, :half], nt, preferred_element_type=jnp.float32)
    s2 = lax.dot_general(k2_ref[0], q[:, half:], nt, preferred_element_type=jnp.float32)
    k = PEER_TOPK
    v1 = _top_values(s1, k)
    v2 = _top_values(s2, k)
    v2s = jnp.concatenate(v2, axis=0)
    cand = [v1[0] + v2s]
    cand += [v1[a] + v2s[:8] for a in range(1, 8)]
    cand += [jnp.concatenate(v1[8:], axis=0) + v2[0]]
    tau = _top_values(jnp.concatenate(cand, axis=0), k)[-1]
    e2v = jnp.exp(v2s - v2[0])
    z = jnp.zeros_like(tau)
    theta = []
    for a in range(k):
        sel = (v1[a] + v2s) >= tau
        theta.append(jnp.min(jnp.where(sel, v2s, jnp.inf), axis=0, keepdims=True))
        z = z + jnp.exp(v1[a] - v1[0]) * jnp.sum(jnp.where(sel, e2v, 0.0), axis=0, keepdims=True)
    th = jnp.full_like(s1, jnp.inf)
    for a in range(k):
        th = jnp.where(s1 == v1[a], theta[a], th)
    th_ref[0] = th
    e1_ref[0] = jnp.where(s1 >= v1[k - 1], jnp.exp(s1 - v1[0]) / z, 0.0)
    s2_ref[0] = s2
    e2_ref[0] = jnp.exp(s2 - v2[0])


def peer_route(x, g, shift, scale, wq_b, k1_b, k2_b, tb):
    n, d = x.shape
    nb = n // tb
    hq = PEER_HEADS
    row = functools.partial(_mod_row, rows_per_block=tb)
    route_shape = jax.ShapeDtypeStruct((hq, N_KEYS, n), jnp.float32)
    route_spec = pl.BlockSpec((1, N_KEYS, tb), lambda i, h: (h, 0, i))
    return pl.pallas_call(
        _peer_route_body,
        grid=(nb, hq),
        in_specs=[
            pl.BlockSpec((tb, d), lambda i, h: (i, 0)),
            pl.BlockSpec((1, d), lambda i, h: (0, 0)),
            pl.BlockSpec((1, 1, d), lambda i, h: (row(i), 0, 0)),
            pl.BlockSpec((1, 1, d), lambda i, h: (row(i), 0, 0)),
            pl.BlockSpec((d, D_QUERY), lambda i, h: (0, h)),
            pl.BlockSpec((1, N_KEYS, D_QUERY // 2), lambda i, h: (h, 0, 0)),
            pl.BlockSpec((1, N_KEYS, D_QUERY // 2), lambda i, h: (h, 0, 0)),
        ],
        out_specs=[pl.BlockSpec((tb, d), lambda i, h: (i, 0)), route_spec, route_spec, route_spec, route_spec],
        out_shape=[jax.ShapeDtypeStruct((n, d), jnp.bfloat16), route_shape, route_shape, route_shape, route_shape],
        scratch_shapes=[pltpu.VMEM((tb, d), jnp.bfloat16)],
        compiler_params=pltpu.CompilerParams(dimension_semantics=("parallel", "arbitrary"),
                                             vmem_limit_bytes=VMEM_LIMIT),
        name="peer_route",
    )(x, g, shift, scale, wq_b, k1_b, k2_b)


def _gelu(x):
    return 0.5 * x * (1.0 + lax.erf(x * np.float32(1.0 / np.sqrt(2.0))))


def _peer_expert_body(x_ref, gate_ref, hb_ref, th_ref, e1_ref, s2_ref, e2_ref, u_ref, vt_ref,
                      o_ref, acc_s, g_s, *, te, tb):
    j = pl.program_id(1)

    @pl.when(j == 0)
    def _():
        acc_s[...] = jnp.zeros_like(acc_s)

    nt = (((1,), (1,)), ((), ()))
    act = _gelu(lax.dot_general(u_ref[...], hb_ref[...], nt, preferred_element_type=jnp.float32))
    lanes = 128
    keys_per_tile = te // N_KEYS
    i0 = pl.multiple_of(j * keys_per_tile, keys_per_tile)
    for c in range(tb // lanes):
        cs = slice(c * lanes, (c + 1) * lanes)
        th = [th_ref[h, pl.ds(i0, keys_per_tile), cs] for h in range(PEER_HEADS)]
        e1 = [e1_ref[h, pl.ds(i0, keys_per_tile), cs] for h in range(PEER_HEADS)]
        for ii in range(keys_per_tile):
            w = jnp.zeros((N_KEYS, lanes), jnp.float32)
            for h in range(PEER_HEADS):
                keep = s2_ref[h, :, cs] >= th[h][ii:ii + 1]
                w = w + e1[h][ii:ii + 1] * jnp.where(keep, e2_ref[h, :, cs], 0.0)
            rs = slice(ii * N_KEYS, (ii + 1) * N_KEYS)
            g_s[rs, cs] = (w * act[rs, cs]).astype(jnp.bfloat16)
    acc_s[...] += jnp.dot(vt_ref[...], g_s[...], preferred_element_type=jnp.float32)

    @pl.when(j == pl.num_programs(1) - 1)
    def _():
        o_ref[...] = x_ref[...] + gate_ref[0] * acc_s[...].T


def peer_experts(x, gate, hb, th, e1, s2, e2, u_b, vt_b, tb, te):
    n, d = x.shape
    nb = n // tb
    ne = N_EXPERTS // te
    row = functools.partial(_mod_row, rows_per_block=tb)
    route_spec = pl.BlockSpec((PEER_HEADS, N_KEYS, tb), lambda i, j: (0, 0, i))
    return pl.pallas_call(
        functools.partial(_peer_expert_body, te=te, tb=tb),
        grid=(nb, ne),
        in_specs=[
            pl.BlockSpec((tb, d), lambda i, j: (i, 0)),
            pl.BlockSpec((1, 1, d), lambda i, j: (row(i), 0, 0)),
            pl.BlockSpec((tb, d), lambda i, j: (i, 0)),
            route_spec, route_spec, route_spec, route_spec,
            pl.BlockSpec((te, d), lambda i, j: (j, 0)),
            pl.BlockSpec((d, te), lambda i, j: (0, j)),
        ],
        out_specs=pl.BlockSpec((tb, d), lambda i, j: (i, 0)),
        out_shape=jax.ShapeDtypeStruct((n, d), jnp.float32),
        scratch_shapes=[pltpu.VMEM((d, tb), jnp.float32), pltpu.VMEM((te, tb), jnp.bfloat16)],
        compiler_params=pltpu.CompilerParams(dimension_semantics=("parallel", "arbitrary"),
                                             vmem_limit_bytes=VMEM_LIMIT),
        name="peer_experts",
    )(x, gate, hb, th, e1, s2, e2, u_b, vt_b)


def peer_layer(x, g, shift, scale, gate, w_q, keys, u_tab, v_tab, tb=512, te=1024):
    bf = jnp.bfloat16
    hb, th, e1, s2, e2 = peer_route(x, g, shift, scale, w_q.astype(bf), keys[0].astype(bf), keys[1].astype(bf), tb)
    return peer_experts(x, gate, hb, th, e1, s2, e2, u_tab.astype(bf), v_tab.T.astype(bf), tb, te)


def _final_rms_body(x_ref, g_ref, o_ref):
    x = x_ref[...]
    o_ref[...] = x * lax.rsqrt(jnp.mean(x * x, axis=-1, keepdims=True) + RMS_EPS) * g_ref[...]


def final_rmsnorm(x, g):
    b, t, d = x.shape
    x2 = x.reshape(b * t, d)
    rows = 512
    out = pl.pallas_call(
        _final_rms_body,
        grid=(b * t // rows,),
        in_specs=[pl.BlockSpec((rows, d), lambda i: (i, 0)), pl.BlockSpec((1, d), lambda i: (0, 0))],
        out_specs=pl.BlockSpec((rows, d), lambda i: (i, 0)),
        out_shape=jax.ShapeDtypeStruct((b * t, d), x.dtype),
    )(x2, g.reshape(1, d))
    return out.reshape(b, t, d)


def kernel(x_prompt, x_sample, cache_k, cache_v, state_rwkv, c, c_ctx, mod_w, mod_b, norm_g,
           ab_w_in, ab_w_out, ab_sink, ab_shift_w, ab_k_k, ab_k_a, ab_r_k, ab_w0, ab_w2,
           ab_a0, ab_a2, ab_g2, ab_gn_g, ab_gn_b, cv_w_pw1, cv_b_pw1, cv_w_dw, cv_b_dw,
           cv_ln_g, cv_ln_b, cv_w_pw2, cv_b_pw2, peer_w_q, peer_keys, peer_u, peer_v, final_g):
    d = D_MODEL
    x = jnp.concatenate([x_prompt.reshape(N_PROMPT, d), x_sample.reshape(DEC_BATCH * DEC_SEQ, d)], axis=0)
    cond = jnp.concatenate([c_ctx[None, :], c], axis=0)
    new_k, new_v, new_s = [], [], []
    for layer in range(DEPTH):
        i = layer // 2
        s1, c1, g1, s2, c2, g2 = modulation(cond, mod_w[layer], mod_b[layer])
        xp = x[:N_PROMPT].reshape(BATCH, SEQ, d)
        xs = x[N_PROMPT:].reshape(DEC_BATCH, DEC_SEQ, d)
        hp = modulate(rmsnorm(xp, norm_g[layer, 0]), s1[:1], c1[:1])
        hs = modulate(rmsnorm(xs, norm_g[layer, 0]), s1[1:], c1[1:])
        if layer % 2 == 0:
            prm = (ab_w_in[i], ab_w_out[i], ab_sink[i], ab_shift_w[i], ab_k_k[i], ab_k_a[i],
                   ab_r_k[i], ab_w0[i], ab_w2[i], ab_a0[i], ab_a2[i], ab_g2[i], ab_gn_g[i], ab_gn_b[i])
            yp, k_c, v_c, s_c = mixer_even(hp, *prm, None)
            ys, _, _, _ = mixer_even(hs, *prm, (cache_k[:, i], cache_v[:, i], state_rwkv[:, i]))
            new_k.append(k_c)
            new_v.append(v_c)
            new_s.append(s_c)
        else:
            prm = (cv_w_pw1[i], cv_b_pw1[i], cv_w_dw[i], cv_b_dw[i], cv_ln_g[i], cv_ln_b[i],
                   cv_w_pw2[i], cv_b_pw2[i])
            yp = mixer_odd(hp, *prm)
            ys = mixer_odd(hs, *prm)
        xp = xp + g1[:1] * yp
        xs = xs + g1[1:] * ys
        x = jnp.concatenate([xp.reshape(N_PROMPT, d), xs.reshape(DEC_BATCH * DEC_SEQ, d)], axis=0)
        x = peer_layer(x, norm_g[layer, 1][None, :], s2, c2, g2,
                       peer_w_q[layer], peer_keys[layer], peer_u[layer], peer_v[layer])
    y_prompt = final_rmsnorm(x[:N_PROMPT].reshape(BATCH, SEQ, d), final_g)
    y_sample = final_rmsnorm(x[N_PROMPT:].reshape(DEC_BATCH, DEC_SEQ, d), final_g)
    new_cache_k = jnp.stack(new_k, axis=1)
    new_cache_v = jnp.stack(new_v, axis=1)
    new_state_rwkv = jnp.stack(new_s, axis=1)
    return (y_prompt, y_sample, new_cache_k, new_cache_v, new_state_rwkv)
```

```python
import functools
import jax
import jax.numpy as jnp
from jax import lax
import numpy as np
from jax.experimental import pallas as pl
from jax.experimental.pallas import tpu as pltpu

D_MODEL = 1024
BATCH = 32
SEQ = 256
DEPTH = 2
DEC_BATCH = 4
DEC_SEQ = 4096
PAST_LEN = 256

GRID_W = 64
BLOCK = 128
WINDOW = 128
ATTN_HEADS = 8
KV_HEADS = 2
GROUP = ATTN_HEADS // KV_HEADS
HEAD_DIM = 64
D_ATTN = ATTN_HEADS * HEAD_DIM
D_ATTN_IN = D_ATTN + 2 * KV_HEADS * HEAD_DIM
RWKV_HEADS = 8
RWKV_HEAD_DIM = 64
D_RWKV = RWKV_HEADS * RWKV_HEAD_DIM
LORA_W = 64
LORA_A = 64
LORA_G = 128
D_RWKV_IN = 3 * D_RWKV + 2 * LORA_W + 2 * LORA_A + LORA_G
D_IN_AB = D_ATTN_IN + D_RWKV_IN
SHIFT_WIDTH = 3
ROPE_BASE = 10000.0
D_CONV = D_MODEL
CONV_WIDTH = 31
PEER_HEADS = 8
N_KEYS = 128
N_EXPERTS = N_KEYS * N_KEYS
D_QUERY = 256
PEER_TOPK = 16
N_EVEN = (DEPTH + 1) // 2
N_ODD = DEPTH // 2
RMS_EPS = 1e-6
LN_EPS = 1e-5
GN_EPS = 64e-5
N_PROMPT = BATCH * SEQ
SAMPLE_LEN = DEC_SEQ
N_TOKENS = N_PROMPT + DEC_BATCH * DEC_SEQ
VMEM_LIMIT = 56 * 1024 * 1024


def rmsnorm(x, g):
    xf = x.astype(jnp.float32)
    y = xf * lax.rsqrt(jnp.mean(xf * xf, axis=-1, keepdims=True) + RMS_EPS)
    return (y * g.astype(jnp.float32)).astype(x.dtype)


def layernorm(x, g, b, eps):
    xf = x.astype(jnp.float32)
    mu = jnp.mean(xf, axis=-1, keepdims=True)
    var = jnp.mean(jnp.square(xf - mu), axis=-1, keepdims=True)
    y = (xf - mu) * lax.rsqrt(var + eps)
    return (y * g.astype(jnp.float32) + b.astype(jnp.float32)).astype(x.dtype)


def modulation(cond, w, b):
    m = jax.nn.silu(cond) @ w + b
    return jnp.split(m[:, None, :], 6, axis=-1)


def modulate(h, shift, scale):
    return h * (1 + scale) + shift


def depthwise_conv(x, w):
    return lax.conv_general_dilated(
        x, w[:, None, :].astype(x.dtype), window_strides=(1,), padding='SAME',
        dimension_numbers=('NWC', 'WIO', 'NWC'), feature_group_count=x.shape[-1])


def axial_rope(x):
    t = x.shape[1]
    rows = t // GRID_W
    row_id = jnp.repeat(jnp.arange(rows, dtype=jnp.float32), GRID_W)
    col_id = jnp.tile(jnp.arange(GRID_W, dtype=jnp.float32), rows)
    half = HEAD_DIM // 2
    quarter = half // 2
    freqs = ROPE_BASE ** (-jnp.arange(quarter, dtype=jnp.float32) / quarter)

    def rotate(xa, pos):
        ang = pos[:, None] * freqs[None, :]
        cos = jnp.cos(ang)[None, :, None, :]
        sin = jnp.sin(ang)[None, :, None, :]
        xa = xa.astype(jnp.float32)
        a, b = xa[..., :quarter], xa[..., quarter:]
        return jnp.concatenate([a * cos - b * sin, b * cos + a * sin], axis=-1)

    y = jnp.concatenate([rotate(x[..., :half], row_id), rotate(x[..., half:], col_id)], axis=-1)
    return y.astype(x.dtype)


def context_attention(q, k, v, sink):
    b, c = q.shape[:2]
    nqb = c // BLOCK
    scale = HEAD_DIM ** -0.5
    qb = q.reshape(b, nqb, BLOCK, KV_HEADS, GROUP, HEAD_DIM).transpose(1, 0, 2, 3, 4, 5)
    sink_l = jnp.broadcast_to(sink.astype(jnp.float32).reshape(1, KV_HEADS, GROUP, 1, 1),
                              (b, KV_HEADS, GROUP, BLOCK, 1))

    def one(qblk):
        s = jnp.einsum('bqkgd,bckd->bkgqc', qblk, k).astype(jnp.float32) * scale
        p = jax.nn.softmax(jnp.concatenate([s, sink_l], axis=-1), axis=-1)[..., :c]
        return jnp.einsum('bkgqc,bckd->bqkgd', p.astype(v.dtype), v)

    out = lax.map(one, qb)
    return out.transpose(1, 0, 2, 3, 4, 5).reshape(b, c, D_ATTN)


def window_attention(q, k, v, k_ctx, v_ctx, sink):
    b, t = q.shape[:2]
    c = k_ctx.shape[1]
    nb = t // BLOCK
    scale = HEAD_DIM ** -0.5
    qb = q.reshape(b, nb, BLOCK, KV_HEADS, GROUP, HEAD_DIM)
    pad = ((0, 0), (BLOCK, BLOCK), (0, 0), (0, 0))
    kp = jnp.pad(k, pad).reshape(b, nb + 2, BLOCK, KV_HEADS, HEAD_DIM)
    vp = jnp.pad(v, pad).reshape(b, nb + 2, BLOCK, KV_HEADS, HEAD_DIM)
    kb = jnp.concatenate([kp[:, :-2], kp[:, 1:-1], kp[:, 2:]], axis=2)
    vb = jnp.concatenate([vp[:, :-2], vp[:, 1:-1], vp[:, 2:]], axis=2)
    qpos = jnp.arange(t).reshape(nb, BLOCK)
    kpos = (jnp.arange(nb)[:, None] - 1) * BLOCK + jnp.arange(3 * BLOCK)[None, :]
    rel = kpos[:, None, :] - qpos[:, :, None]
    valid = (jnp.abs(rel) <= WINDOW) & (kpos[:, None, :] >= 0) & (kpos[:, None, :] < t)
    s_loc = jnp.einsum('bnqkgd,bnskd->bnkgqs', qb, kb).astype(jnp.float32) * scale
    s_loc = jnp.where(valid[None, :, None, None], s_loc, -jnp.inf)
    s_ctx = jnp.einsum('bnqkgd,bckd->bnkgqc', qb, k_ctx).astype(jnp.float32) * scale
    sink_l = jnp.broadcast_to(sink.astype(jnp.float32).reshape(1, 1, KV_HEADS, GROUP, 1, 1),
                              s_loc.shape[:-1] + (1,))
    p = jax.nn.softmax(jnp.concatenate([s_loc, s_ctx, sink_l], axis=-1), axis=-1)
    p_loc = p[..., :3 * BLOCK].astype(v.dtype)
    p_ctx = p[..., 3 * BLOCK:3 * BLOCK + c].astype(v.dtype)
    o = (jnp.einsum('bnkgqs,bnskd->bnqkgd', p_loc, vb)
         + jnp.einsum('bnkgqc,bckd->bnqkgd', p_ctx, v_ctx))
    return o.reshape(b, t, D_ATTN)


SUBLANES = 8
LANES = 128
HEAD_PAIRS = RWKV_HEADS // 2
SCAN_STEPS = 128


def _seg_sums(x, seg_a):
    return (jnp.sum(jnp.where(seg_a, x, 0.0), axis=1, keepdims=True),
            jnp.sum(jnp.where(seg_a, 0.0, x), axis=1, keepdims=True))


def _rwkv_scan_body(r_f, kh_f, v_f, w_f, ka_f, kt_f, r_b, kh_b, v_b, w_b, ka_b, kt_b, s0_ref,
                    yf_ref, yb_ref, sfin_ref, s_scr, yacc, vt_scr):
    tb = pl.program_id(1)
    n = RWKV_HEAD_DIM
    tt = SCAN_STEPS

    @pl.when(tb == 0)
    def _():
        s_scr[...] = s0_ref[0]

    ins = ((r_f, kh_f, v_f, w_f, ka_f, kt_f), (r_b, kh_b, v_b, w_b, ka_b, kt_b))
    for d in range(2):
        for p in range(HEAD_PAIRS):
            vt_scr[d, p] = ins[d][2][0, :, p * LANES:(p + 1) * LANES].T

    lane = lax.broadcasted_iota(jnp.int32, (n, LANES), 1)
    seg_a = lane < n
    lane2 = lax.broadcasted_iota(jnp.int32, (LANES, LANES), 1)
    groups = tt // SUBLANES

    def group(g, carry):
        bases = (pl.multiple_of(g * SUBLANES, SUBLANES), pl.multiple_of((groups - 1 - g) * SUBLANES, SUBLANES))
        rows = [[ref[0, pl.ds(bases[d], SUBLANES), :] for ref in (ins[d][0], ins[d][1], ins[d][3], ins[d][4], ins[d][5])]
                for d in range(2)]
        pairs = range(HEAD_PAIRS)
        for step in range(SUBLANES):
            for d in range(2):
                ii = step if d == 0 else SUBLANES - 1 - step
                at_t = lane2 == bases[d] + ii
                vec = [[x[ii:ii + 1, p * LANES:(p + 1) * LANES] for x in rows[d]] for p in pairs]
                s_old = [s_scr[d, p] for p in pairs]
                sk = [_seg_sums(s_old[p] * vec[p][1], seg_a) for p in pairs]
                vcol = [jnp.sum(jnp.where(at_t, vt_scr[d, p], 0.0), axis=1, keepdims=True) for p in pairs]
                ysum = []
                for p in pairs:
                    r_t, _, w_t, ka_t, kt_t = vec[p]
                    vc = jnp.where(seg_a, vcol[p][:n], vcol[p][n:])
                    s = s_old[p] * w_t - jnp.where(seg_a, *sk[p]) * ka_t + vc * kt_t
                    s_scr[d, p] = s
                    ysum.append(_seg_sums(s * r_t, seg_a))
                for p in pairs:
                    ya, yb = ysum[p]
                    y = jnp.concatenate([jnp.broadcast_to(ya, (n, LANES)), jnp.broadcast_to(yb, (n, LANES))], axis=0)
                    yacc[d, p] = jnp.where(at_t, y, yacc[d, p])
        return carry

    lax.fori_loop(0, groups, group, 0)
    for p in range(HEAD_PAIRS):
        yf_ref[0, :, p * LANES:(p + 1) * LANES] = yacc[0, p].T
        yb_ref[0, :, p * LANES:(p + 1) * LANES] = yacc[1, p].T

    @pl.when(tb == pl.num_programs(1) - 1)
    def _():
        sfin_ref[0] = s_scr[...]


def rwkv_scan_pallas(r, kh, v, w_f, ka_f, kt_f, w_b, ka_b, kt_b, s0p):
    b, t, dm = r.shape
    tt = SCAN_STEPS
    nt = t // tt
    fwd = pl.BlockSpec((1, tt, dm), lambda i, j: (i, j, 0))
    bwd = pl.BlockSpec((1, tt, dm), lambda i, j: (i, nt - 1 - j, 0))
    st = pl.BlockSpec((1, 2, HEAD_PAIRS, RWKV_HEAD_DIM, LANES), lambda i, j: (i, 0, 0, 0, 0))
    return pl.pallas_call(
        _rwkv_scan_body,
        grid=(b, nt),
        in_specs=[fwd] * 6 + [bwd] * 6 + [st],
        out_specs=[fwd, bwd, st],
        out_shape=[jax.ShapeDtypeStruct((b, t, dm), jnp.float32), jax.ShapeDtypeStruct((b, t, dm), jnp.float32),
                   jax.ShapeDtypeStruct(s0p.shape, jnp.float32)],
        scratch_shapes=[pltpu.VMEM((2, HEAD_PAIRS, RWKV_HEAD_DIM, LANES), jnp.float32),
                        pltpu.VMEM((2, HEAD_PAIRS, LANES, LANES), jnp.float32),
                        pltpu.VMEM((2, HEAD_PAIRS, LANES, tt), jnp.float32)],
        compiler_params=pltpu.CompilerParams(dimension_semantics=("parallel", "arbitrary"),
                                             vmem_limit_bytes=VMEM_LIMIT),
        name="rwkv_scan",
    )(r, kh, v, w_f, ka_f, kt_f, r, kh, v, w_b, ka_b, kt_b, s0p)


def pack_state(s):
    b = s.shape[0]
    s = s.reshape(b, 2, HEAD_PAIRS, 2, RWKV_HEAD_DIM, RWKV_HEAD_DIM)
    return s.transpose(0, 1, 2, 4, 3, 5).reshape(b, 2, HEAD_PAIRS, RWKV_HEAD_DIM, LANES)


def unpack_state(sp):
    b = sp.shape[0]
    s = sp.reshape(b, 2, HEAD_PAIRS, RWKV_HEAD_DIM, 2, RWKV_HEAD_DIM)
    return s.transpose(0, 1, 2, 4, 3, 5).reshape(b, 2, RWKV_HEADS, RWKV_HEAD_DIM, RWKV_HEAD_DIM)


def rwkv_mix(u, k_k, k_a, r_k, w0, w2, a0, a2, g2, gn_g, gn_b, s0):
    f32 = jnp.float32
    b, t = u.shape[:2]
    u = u.astype(f32)
    cuts = [D_RWKV, 2 * D_RWKV, 3 * D_RWKV, 3 * D_RWKV + LORA_W, 3 * D_RWKV + 2 * LORA_W,
            3 * D_RWKV + 2 * LORA_W + LORA_A, 3 * D_RWKV + 2 * LORA_W + 2 * LORA_A]
    r, k, v, wl_f, wl_b, al_f, al_b, gl = jnp.split(u, cuts, axis=-1)

    def heads(z):
        return z.reshape(b, t, RWKV_HEADS, RWKV_HEAD_DIM)

    kk = heads(k * k_k)
    kh = kk / jnp.maximum(jnp.linalg.norm(kk, axis=-1, keepdims=True), 1e-12)
    s0 = s0.astype(f32)

    def direction(idx, wl, al):
        logw = -jax.nn.softplus(-(w0[idx] + jnp.tanh(wl) @ w2[idx])) - 0.5
        decay = jnp.exp(-jnp.exp(logw))
        a = jax.nn.sigmoid(a0[idx] + al @ a2[idx])
        kt = k * (1 + (a - 1) * k_a)
        return decay, kh.reshape(b, t, D_RWKV) * a, kt

    w_f, ka_f, kt_f = direction(0, wl_f, al_f)
    w_b, ka_b, kt_b = direction(1, wl_b, al_b)
    y_f, y_b, s_fin = rwkv_scan_pallas(r, kh.reshape(b, t, D_RWKV), v, w_f, ka_f, kt_f, w_b, ka_b, kt_b,
                                       pack_state(s0))
    s_fin = unpack_state(s_fin)
    kt_f, kt_b = heads(kt_f), heads(kt_b)
    y = layernorm(heads(y_f + y_b), gn_g.reshape(RWKV_HEADS, RWKV_HEAD_DIM),
                  gn_b.reshape(RWKV_HEADS, RWKV_HEAD_DIM), GN_EPS)
    bonus = jnp.sum(heads(r) * 0.5 * (kt_f + kt_b) * r_k.reshape(RWKV_HEADS, RWKV_HEAD_DIM),
                    axis=-1, keepdims=True) * heads(v)
    g = jax.nn.sigmoid(gl) @ g2
    out = (y + bonus).reshape(b, t, D_RWKV) * g
    return out, s_fin


def mixer_even(h, w_in, w_out, sink, shift_w, k_k, k_a, r_k, w0, w2, a0, a2, g2, gn_g, gn_b, ctx):
    b, t = h.shape[:2]
    d_kv = KV_HEADS * HEAD_DIM
    u = h @ w_in
    q = u[..., :D_ATTN].reshape(b, t, ATTN_HEADS, HEAD_DIM)
    k = u[..., D_ATTN:D_ATTN + d_kv].reshape(b, t, KV_HEADS, HEAD_DIM)
    v = u[..., D_ATTN + d_kv:D_ATTN_IN].reshape(b, t, KV_HEADS, HEAD_DIM)
    u_r = depthwise_conv(u[..., D_ATTN_IN:], shift_w)
    if ctx is None:
        a_out = context_attention(q, k, v, sink)
        s0 = jnp.zeros((b, 2, RWKV_HEADS, RWKV_HEAD_DIM, RWKV_HEAD_DIM), jnp.float32)
    else:
        k_ctx, v_ctx, s0 = ctx
        a_out = window_attention(axial_rope(q), axial_rope(k), v,
                                 k_ctx.astype(h.dtype), v_ctx.astype(h.dtype), sink)
    b_out, s_fin = rwkv_mix(u_r, k_k, k_a, r_k, w0, w2, a0, a2, g2, gn_g, gn_b, s0)
    y = jnp.concatenate([a_out, b_out.astype(a_out.dtype)], axis=-1) @ w_out
    return y, k, v, s_fin


def mixer_odd(h, w_pw1, b_pw1, w_dw, b_dw, ln_g, ln_b, w_pw2, b_pw2):
    u = h @ w_pw1 + b_pw1
    u = u[..., :D_CONV] * jax.nn.sigmoid(u[..., D_CONV:])
    u = depthwise_conv(u, w_dw) + b_dw
    u = jax.nn.silu(layernorm(u, ln_g, ln_b, LN_EPS))
    return u @ w_pw2 + b_pw2


def _mod_row(i, rows_per_block):
    tok = i * rows_per_block
    return jnp.where(tok < N_PROMPT, 0, 1 + (tok - N_PROMPT) // SAMPLE_LEN)


def _norm_mod(x, g, shift, scale):
    y = x * lax.rsqrt(jnp.mean(x * x, axis=-1, keepdims=True) + RMS_EPS)
    return (y * g) * (1 + scale) + shift


def _top_values(s, n):
    vals = []
    cur = s
    for a in range(n):
        m = jnp.max(cur, axis=0, keepdims=True)
        vals.append(m)
        if a + 1 < n:
            cur = jnp.where(cur == m, -jnp.inf, cur)
    return vals


def _peer_route_body(x_ref, g_ref, shift_ref, scale_ref, wq_ref, k1_ref, k2_ref,
                     hb_ref, th_ref, e1_ref, s2_ref, e2_ref, hb_s):
    h = pl.program_id(1)

    @pl.when(h == 0)
    def _():
        hn = _norm_mod(x_ref[...], g_ref[...], shift_ref[0], scale_ref[0])
        hb_s[...] = hn.astype(jnp.bfloat16)
        hb_ref[...] = hn.astype(jnp.bfloat16)

    half = D_QUERY // 2
    q = jnp.dot(hb_s[...], wq_ref[...], preferred_element_type=jnp.float32).astype(jnp.bfloat16)
    nt = (((1,), (1,)), ((), ()))
    s1 = lax.dot_general(k1_ref[0], q[:, :half], nt, preferred_element_type=jnp.float32)
    s2 = lax.dot_general(k2_ref[0], q[:, half:], nt, preferred_element_type=jnp.float32)
    k = PEER_TOPK
    v1 = _top_values(s1, k)
    v2 = _top_values(s2, k)
    v2s = jnp.concatenate(v2, axis=0)
    cand = [v1[0] + v2s]
    cand += [v1[a] + v2s[:8] for a in range(1, 8)]
    cand += [jnp.concatenate(v1[8:], axis=0) + v2[0]]
    tau = _top_values(jnp.concatenate(cand, axis=0), k)[-1]
    e2v = jnp.exp(v2s - v2[0])
    z = jnp.zeros_like(tau)
    theta = []
    for a in range(k):
        sel = (v1[a] + v2s) >= tau
        theta.append(jnp.min(jnp.where(sel, v2s, jnp.inf), axis=0, keepdims=True))
        z = z + jnp.exp(v1[a] - v1[0]) * jnp.sum(jnp.where(sel, e2v, 0.0), axis=0, keepdims=True)
    th = jnp.full_like(s1, jnp.inf)
    for a in range(k):
        th = jnp.where(s1 == v1[a], theta[a], th)
    th_ref[0] = th
    e1_ref[0] = jnp.where(s1 >= v1[k - 1], jnp.exp(s1 - v1[0]) / z, 0.0)
    s2_ref[0] = s2
    e2_ref[0] = jnp.exp(s2 - v2[0])


def peer_route(x, g, shift, scale, wq_b, k1_b, k2_b, tb):
    n, d = x.shape
    nb = n // tb
    hq = PEER_HEADS
    row = functools.partial(_mod_row, rows_per_block=tb)
    route_shape = jax.ShapeDtypeStruct((hq, N_KEYS, n), jnp.float32)
    route_spec = pl.BlockSpec((1, N_KEYS, tb), lambda i, h: (h, 0, i))
    return pl.pallas_call(
        _peer_route_body,
        grid=(nb, hq),
        in_specs=[
            pl.BlockSpec((tb, d), lambda i, h: (i, 0)),
            pl.BlockSpec((1, d), lambda i, h: (0, 0)),
            pl.BlockSpec((1, 1, d), lambda i, h: (row(i), 0, 0)),
            pl.BlockSpec((1, 1, d), lambda i, h: (row(i), 0, 0)),
            pl.BlockSpec((d, D_QUERY), lambda i, h: (0, h)),
            pl.BlockSpec((1, N_KEYS, D_QUERY // 2), lambda i, h: (h, 0, 0)),
            pl.BlockSpec((1, N_KEYS, D_QUERY // 2), lambda i, h: (h, 0, 0)),
        ],
        out_specs=[pl.BlockSpec((tb, d), lambda i, h: (i, 0)), route_spec, route_spec, route_spec, route_spec],
        out_shape=[jax.ShapeDtypeStruct((n, d), jnp.bfloat16), route_shape, route_shape, route_shape, route_shape],
        scratch_shapes=[pltpu.VMEM((tb, d), jnp.bfloat16)],
        compiler_params=pltpu.CompilerParams(dimension_semantics=("parallel", "arbitrary"),
                                             vmem_limit_bytes=VMEM_LIMIT),
        name="peer_route",
    )(x, g, shift, scale, wq_b, k1_b, k2_b)


def _gelu(x):
    return 0.5 * x * (1.0 + lax.erf(x * np.float32(1.0 / np.sqrt(2.0))))


def _peer_expert_body(x_ref, gate_ref, hb_ref, th_ref, e1_ref, s2_ref, e2_ref, u_ref, vt_ref,
                      o_ref, acc_s, g_s, *, te, tb):
    j = pl.program_id(1)

    @pl.when(j == 0)
    def _():
        acc_s[...] = jnp.zeros_like(acc_s)

    nt = (((1,), (1,)), ((), ()))
    act = _gelu(lax.dot_general(u_ref[...], hb_ref[...], nt, preferred_element_type=jnp.float32))
    lanes = 128
    keys_per_tile = te // N_KEYS
    i0 = pl.multiple_of(j * keys_per_tile, keys_per_tile)
    for c in range(tb // lanes):
        cs = slice(c * lanes, (c + 1) * lanes)
        th = [th_ref[h, pl.ds(i0, keys_per_tile), cs] for h in range(PEER_HEADS)]
        e1 = [e1_ref[h, pl.ds(i0, keys_per_tile), cs] for h in range(PEER_HEADS)]
        for ii in range(keys_per_tile):
            w = jnp.zeros((N_KEYS, lanes), jnp.float32)
            for h in range(PEER_HEADS):
                keep = s2_ref[h, :, cs] >= th[h][ii:ii + 1]
                w = w + e1[h][ii:ii + 1] * jnp.where(keep, e2_ref[h, :, cs], 0.0)
            rs = slice(ii * N_KEYS, (ii + 1) * N_KEYS)
            g_s[rs, cs] = (w * act[rs, cs]).astype(jnp.bfloat16)
    acc_s[...] += jnp.dot(vt_ref[...], g_s[...], preferred_element_type=jnp.float32)

    @pl.when(j == pl.num_programs(1) - 1)
    def _():
        o_ref[...] = x_ref[...] + gate_ref[0] * acc_s[...].T


def peer_experts(x, gate, hb, th, e1, s2, e2, u_b, vt_b, tb, te):
    n, d = x.shape
    nb = n // tb
    ne = N_EXPERTS // te
    row = functools.partial(_mod_row, rows_per_block=tb)
    route_spec = pl.BlockSpec((PEER_HEADS, N_KEYS, tb), lambda i, j: (0, 0, i))
    return pl.pallas_call(
        functools.partial(_peer_expert_body, te=te, tb=tb),
        grid=(nb, ne),
        in_specs=[
            pl.BlockSpec((tb, d), lambda i, j: (i, 0)),
            pl.BlockSpec((1, 1, d), lambda i, j: (row(i), 0, 0)),
            pl.BlockSpec((tb, d), lambda i, j: (i, 0)),
            route_spec, route_spec, route_spec, route_spec,
            pl.BlockSpec((te, d), lambda i, j: (j, 0)),
            pl.BlockSpec((d, te), lambda i, j: (0, j)),
        ],
        out_specs=pl.BlockSpec((tb, d), lambda i, j: (i, 0)),
        out_shape=jax.ShapeDtypeStruct((n, d), jnp.float32),
        scratch_shapes=[pltpu.VMEM((d, tb), jnp.float32), pltpu.VMEM((te, tb), jnp.bfloat16)],
        compiler_params=pltpu.CompilerParams(dimension_semantics=("parallel", "arbitrary"),
                                             vmem_limit_bytes=VMEM_LIMIT),
        name="peer_experts",
    )(x, gate, hb, th, e1, s2, e2, u_b, vt_b)


def peer_layer(x, g, shift, scale, gate, w_q, keys, u_tab, v_tab, tb=512, te=1024):
    bf = jnp.bfloat16
    hb, th, e1, s2, e2 = peer_route(x, g, shift, scale, w_q.astype(bf), keys[0].astype(bf), keys[1].astype(bf), tb)
    return peer_experts(x, gate, hb, th, e1, s2, e2, u_tab.astype(bf), v_tab.T.astype(bf), tb, te)


def _final_rms_body(x_ref, g_ref, o_ref):
    x = x_ref[...]
    o_ref[...] = x * lax.rsqrt(jnp.mean(x * x, axis=-1, keepdims=True) + RMS_EPS) * g_ref[...]


def final_rmsnorm(x, g):
    b, t, d = x.shape
    x2 = x.reshape(b * t, d)
    rows = 512
    out = pl.pallas_call(
        _final_rms_body,
        grid=(b * t // rows,),
        in_specs=[pl.BlockSpec((rows, d), lambda i: (i, 0)), pl.BlockSpec((1, d), lambda i: (0, 0))],
        out_specs=pl.BlockSpec((rows, d), lambda i: (i, 0)),
        out_shape=jax.ShapeDtypeStruct((b * t, d), x.dtype),
    )(x2, g.reshape(1, d))
    return out.reshape(b, t, d)


def kernel(x_prompt, x_sample, cache_k, cache_v, state_rwkv, c, c_ctx, mod_w, mod_b, norm_g,
           ab_w_in, ab_w_out, ab_sink, ab_shift_w, ab_k_k, ab_k_a, ab_r_k, ab_w0, ab_w2,
           ab_a0, ab_a2, ab_g2, ab_gn_g, ab_gn_b, cv_w_pw1, cv_b_pw1, cv_w_dw, cv_b_dw,
           cv_ln_g, cv_ln_b, cv_w_pw2, cv_b_pw2, peer_w_q, peer_keys, peer_u, peer_v, final_g):
    d = D_MODEL
    x = jnp.concatenate([x_prompt.reshape(N_PROMPT, d), x_sample.reshape(DEC_BATCH * DEC_SEQ, d)], axis=0)
    cond = jnp.concatenate([c_ctx[None, :], c], axis=0)
    new_k, new_v, new_s = [], [], []
    for layer in range(DEPTH):
        i = layer // 2
        s1, c1, g1, s2, c2, g2 = modulation(cond, mod_w[layer], mod_b[layer])
        xp = x[:N_PROMPT].reshape(BATCH, SEQ, d)
        xs = x[N_PROMPT:].reshape(DEC_BATCH, DEC_SEQ, d)
        hp = modulate(rmsnorm(xp, norm_g[layer, 0]), s1[:1], c1[:1])
        hs = modulate(rmsnorm(xs, norm_g[layer, 0]), s1[1:], c1[1:])
        if layer % 2 == 0:
            prm = (ab_w_in[i], ab_w_out[i], ab_sink[i], ab_shift_w[i], ab_k_k[i], ab_k_a[i],
                   ab_r_k[i], ab_w0[i], ab_w2[i], ab_a0[i], ab_a2[i], ab_g2[i], ab_gn_g[i], ab_gn_b[i])
            yp, k_c, v_c, s_c = mixer_even(hp, *prm, None)
            ys, _, _, _ = mixer_even(hs, *prm, (cache_k[:, i], cache_v[:, i], state_rwkv[:, i]))
            new_k.append(k_c)
            new_v.append(v_c)
            new_s.append(s_c)
        else:
            prm = (cv_w_pw1[i], cv_b_pw1[i], cv_w_dw[i], cv_b_dw[i], cv_ln_g[i], cv_ln_b[i],
                   cv_w_pw2[i], cv_b_pw2[i])
            yp = mixer_odd(hp, *prm)
            ys = mixer_odd(hs, *prm)
        xp = xp + g1[:1] * yp
        xs = xs + g1[1:] * ys
        x = jnp.concatenate([xp.reshape(N_PROMPT, d), xs.reshape(DEC_BATCH * DEC_SEQ, d)], axis=0)
        x = peer_layer(x, norm_g[layer, 1][None, :], s2, c2, g2,
                       peer_w_q[layer], peer_keys[layer], peer_u[layer], peer_v[layer])
    y_prompt = final_rmsnorm(x[:N_PROMPT].reshape(BATCH, SEQ, d), final_g)
    y_sample = final_rmsnorm(x[N_PROMPT:].reshape(DEC_BATCH, DEC_SEQ, d), final_g)
    new_cache_k = jnp.stack(new_k, axis=1)
    new_cache_v = jnp.stack(new_v, axis=1)
    new_state_rwkv = jnp.stack(new_s, axis=1)
    return (y_prompt, y_sample, new_cache_k, new_cache_v, new_state_rwkv)
```

```python
import functools
import jax
import jax.numpy as jnp
from jax import lax
import numpy as np
from jax.experimental import pallas as pl
from jax.experimental.pallas import tpu as pltpu

D_MODEL = 1024
BATCH = 32
SEQ = 256
DEPTH = 2
DEC_BATCH = 4
DEC_SEQ = 4096
PAST_LEN = 256

GRID_W = 64
BLOCK = 128
WINDOW = 128
ATTN_HEADS = 8
KV_HEADS = 2
GROUP = ATTN_HEADS // KV_HEADS
HEAD_DIM = 64
D_ATTN = ATTN_HEADS * HEAD_DIM
D_ATTN_IN = D_ATTN + 2 * KV_HEADS * HEAD_DIM
RWKV_HEADS = 8
RWKV_HEAD_DIM = 64
D_RWKV = RWKV_HEADS * RWKV_HEAD_DIM
LORA_W = 64
LORA_A = 64
LORA_G = 128
D_RWKV_IN = 3 * D_RWKV + 2 * LORA_W + 2 * LORA_A + LORA_G
D_IN_AB = D_ATTN_IN + D_RWKV_IN
SHIFT_WIDTH = 3
ROPE_BASE = 10000.0
D_CONV = D_MODEL
CONV_WIDTH = 31
PEER_HEADS = 8
N_KEYS = 128
N_EXPERTS = N_KEYS * N_KEYS
D_QUERY = 256
PEER_TOPK = 16
N_EVEN = (DEPTH + 1) // 2
N_ODD = DEPTH // 2
RMS_EPS = 1e-6
LN_EPS = 1e-5
GN_EPS = 64e-5
N_PROMPT = BATCH * SEQ
SAMPLE_LEN = DEC_SEQ
N_TOKENS = N_PROMPT + DEC_BATCH * DEC_SEQ
D_KV = KV_HEADS * HEAD_DIM
VMEM_LIMIT = 56 * 1024 * 1024
SUBLANES = 8
LANES = 128


def _mod_row(i, rows_per_block):
    tok = i * rows_per_block
    return jnp.where(tok < N_PROMPT, 0, 1 + (tok - N_PROMPT) // SAMPLE_LEN)


def _norm_mod(x, g, shift, scale):
    y = x * lax.rsqrt(jnp.mean(x * x, axis=-1, keepdims=True) + RMS_EPS)
    return (y * g) * (1 + scale) + shift


def _proj_body(*refs, prologue, has_bias, residual, splits):
    it = iter(refs)
    x_ref, w_ref = next(it), next(it)
    x = x_ref[...]
    if prologue == "norm_mod":
        g_ref, shift_ref, scale_ref = next(it), next(it), next(it)
        x = _norm_mod(x, g_ref[...], shift_ref[0], scale_ref[0])
    elif prologue == "silu":
        x = x * jax.nn.sigmoid(x)
    acc = jnp.dot(x.astype(jnp.bfloat16), w_ref[...], preferred_element_type=jnp.float32)
    if has_bias:
        acc = acc + next(it)[...]
    if residual:
        res_ref, gate_ref = next(it), next(it)
        acc = res_ref[...] + gate_ref[0] * acc
    start = 0
    for width in splits:
        next(it)[...] = acc[:, start:start + width]
        start += width


def fused_proj(x, w, *, norm_mod=None, silu=False, bias=None, residual=None, splits=None, tm=512, name="proj"):
    m, k = x.shape
    nout = w.shape[1]
    tm = min(tm, m)
    row = functools.partial(_mod_row, rows_per_block=tm)
    args = [x, w.astype(jnp.bfloat16)]
    specs = [pl.BlockSpec((tm, k), lambda i: (i, 0)), pl.BlockSpec((k, nout), lambda i: (0, 0))]
    prologue = None
    if norm_mod is not None:
        prologue = "norm_mod"
        args += list(norm_mod)
        specs += [pl.BlockSpec((1, k), lambda i: (0, 0)),
                  pl.BlockSpec((1, 1, k), lambda i: (row(i), 0, 0)),
                  pl.BlockSpec((1, 1, k), lambda i: (row(i), 0, 0))]
    elif silu:
        prologue = "silu"
    if bias is not None:
        args.append(bias.reshape(1, nout))
        specs.append(pl.BlockSpec((1, nout), lambda i: (0, 0)))
    if residual is not None:
        args += list(residual)
        specs += [pl.BlockSpec((tm, nout), lambda i: (i, 0)), pl.BlockSpec((1, 1, nout), lambda i: (row(i), 0, 0))]
    widths = (nout,) if splits is None else tuple(splits)
    assert sum(widths) == nout
    outs = pl.pallas_call(
        functools.partial(_proj_body, prologue=prologue, has_bias=bias is not None, residual=residual is not None,
                          splits=widths),
        grid=(m // tm,),
        in_specs=specs,
        out_specs=[pl.BlockSpec((tm, wd), lambda i: (i, 0)) for wd in widths],
        out_shape=[jax.ShapeDtypeStruct((m, wd), jnp.float32) for wd in widths],
        compiler_params=pltpu.CompilerParams(dimension_semantics=("parallel",), vmem_limit_bytes=VMEM_LIMIT),
        name=name,
    )(*args)
    return outs[0] if splits is None else outs


def _attend(q, keys, values, masks, sink):
    nt = (((1,), (1,)), ((), ()))
    scale = HEAD_DIM ** -0.5
    ss = []
    for k, msk in zip(keys, masks):
        s = lax.dot_general(q, k, nt, preferred_element_type=jnp.float32) * scale
        ss.append(s if msk is None else jnp.where(msk, s, -jnp.inf))
    m = sink
    for s in ss:
        m = jnp.maximum(m, jnp.max(s, axis=1, keepdims=True))
    ps = [jnp.exp(s - m) for s in ss]
    denom = jnp.exp(sink - m)
    for p in ps:
        denom = denom + jnp.sum(p, axis=1, keepdims=True)
    out = None
    for p, v in zip(ps, values):
        o = jnp.dot((p / denom).astype(jnp.bfloat16), v, preferred_element_type=jnp.float32)
        out = o if out is None else out + o
    return out


def _ctx_attn_body(q_ref, k_ref, v_ref, sink_ref, o_ref):
    bf = jnp.bfloat16
    q = q_ref[...].astype(bf)
    k = k_ref[...].astype(bf)
    v = v_ref[...].astype(bf)
    for h in range(ATTN_HEADS):
        kv = h // GROUP
        ks = slice(kv * HEAD_DIM, (kv + 1) * HEAD_DIM)
        hs = slice(h * HEAD_DIM, (h + 1) * HEAD_DIM)
        o_ref[:, hs] = _attend(q[:, hs], [k[:, ks]], [v[:, ks]], [None], sink_ref[0:1, h:h + 1])


def context_attention_pallas(u, sink):
    kcol = D_ATTN // D_KV
    return pl.pallas_call(
        _ctx_attn_body,
        grid=(BATCH,),
        in_specs=[pl.BlockSpec((SEQ, D_ATTN), lambda i: (i, 0)),
                  pl.BlockSpec((SEQ, D_KV), lambda i: (i, kcol)),
                  pl.BlockSpec((SEQ, D_KV), lambda i: (i, kcol + 1)),
                  pl.BlockSpec((1, ATTN_HEADS), lambda i: (0, 0))],
        out_specs=pl.BlockSpec((SEQ, D_ATTN), lambda i: (i, 0)),
        out_shape=jax.ShapeDtypeStruct((N_PROMPT, D_ATTN), jnp.float32),
        compiler_params=pltpu.CompilerParams(dimension_semantics=("parallel",), vmem_limit_bytes=VMEM_LIMIT),
        name="context_attention",
    )(u, u, u, sink.reshape(1, ATTN_HEADS))


def rope_tables():
    t = np.arange(DEC_SEQ)
    quarter = HEAD_DIM // 4
    freqs = jnp.asarray(ROPE_BASE, jnp.float32) ** (-jnp.arange(quarter, dtype=jnp.float32) / quarter)
    row = jnp.asarray(t // GRID_W, jnp.float32)[:, None] * freqs[None, :]
    col = jnp.asarray(t % GRID_W, jnp.float32)[:, None] * freqs[None, :]
    cos = jnp.concatenate([jnp.cos(row), jnp.cos(row), jnp.cos(col), jnp.cos(col)], axis=1)
    sin = jnp.concatenate([-jnp.sin(row), jnp.sin(row), -jnp.sin(col), jnp.sin(col)], axis=1)
    return cos, sin


def _rope(x, cos, sin, first):
    quarter = HEAD_DIM // 4
    width = x.shape[1]
    partner = jnp.where(first, pltpu.roll(x, width - quarter, 1), pltpu.roll(x, quarter, 1))
    return x * cos + partner * sin


def _win_attn_body(q_ref, kp_ref, kc_ref, kn_ref, vp_ref, vc_ref, vn_ref, kx_ref, vx_ref,
                   cq_ref, sq_ref, cp_ref, sp_ref, cn_ref, sn_ref, sink_ref, o_ref):
    bf = jnp.bfloat16
    n = pl.program_id(1)
    nb = pl.num_programs(1)
    quarter = HEAD_DIM // 4
    lane_q = lax.broadcasted_iota(jnp.int32, (BLOCK, D_ATTN), 1)
    lane_k = lax.broadcasted_iota(jnp.int32, (BLOCK, D_KV), 1)
    cq, sq = cq_ref[...], sq_ref[...]
    tile_q = lambda a: jnp.concatenate([a] * ATTN_HEADS, axis=1)
    tile_k = lambda a: jnp.concatenate([a] * KV_HEADS, axis=1)
    q = _rope(q_ref[...], tile_q(cq), tile_q(sq), (lane_q % (2 * quarter)) < quarter).astype(bf)
    first_k = (lane_k % (2 * quarter)) < quarter
    kp = _rope(kp_ref[...], tile_k(cp_ref[...]), tile_k(sp_ref[...]), first_k).astype(bf)
    kc = _rope(kc_ref[...], tile_k(cq), tile_k(sq), first_k).astype(bf)
    kn = _rope(kn_ref[...], tile_k(cn_ref[...]), tile_k(sn_ref[...]), first_k).astype(bf)
    vp, vc, vn = vp_ref[...].astype(bf), vc_ref[...].astype(bf), vn_ref[...].astype(bf)
    kx, vx = kx_ref[0].astype(bf), vx_ref[0].astype(bf)
    qi = lax.broadcasted_iota(jnp.int32, (BLOCK, BLOCK), 0)
    si = lax.broadcasted_iota(jnp.int32, (BLOCK, BLOCK), 1)
    m_prev = (si >= qi) & (n > 0)
    m_next = (si <= qi) & (n + 1 < nb)
    for h in range(ATTN_HEADS):
        kv = h // GROUP
        ks = slice(kv * HEAD_DIM, (kv + 1) * HEAD_DIM)
        hs = slice(h * HEAD_DIM, (h + 1) * HEAD_DIM)
        o_ref[:, hs] = _attend(q[:, hs], [kp[:, ks], kc[:, ks], kn[:, ks], kx[:, ks]],
                               [vp[:, ks], vc[:, ks], vn[:, ks], vx[:, ks]],
                               [m_prev, None, m_next, None], sink_ref[0:1, h:h + 1])


def window_attention_pallas(u, k_ctx, v_ctx, sink):
    nb = DEC_SEQ // BLOCK
    base = N_PROMPT // BLOCK
    kcol = D_ATTN // D_KV
    cos, sin = rope_tables()
    cur = lambda b, n: base + b * nb + n
    prev = lambda b, n: base + b * nb + jnp.maximum(n - 1, 0)
    nxt = lambda b, n: base + b * nb + jnp.minimum(n + 1, nb - 1)
    blk = lambda rowf, col, width: pl.BlockSpec((BLOCK, width), lambda b, n: (rowf(b, n), col))
    tab = lambda f: pl.BlockSpec((BLOCK, HEAD_DIM), lambda b, n: (f(n), 0))
    t_cur = lambda n: n
    t_prev = lambda n: jnp.maximum(n - 1, 0)
    t_next = lambda n: jnp.minimum(n + 1, nb - 1)
    ctx = pl.BlockSpec((1, PAST_LEN, D_KV), lambda b, n: (b, 0, 0))
    return pl.pallas_call(
        _win_attn_body,
        grid=(DEC_BATCH, nb),
        in_specs=[blk(cur, 0, D_ATTN),
                  blk(prev, kcol, D_KV), blk(cur, kcol, D_KV), blk(nxt, kcol, D_KV),
                  blk(prev, kcol + 1, D_KV), blk(cur, kcol + 1, D_KV), blk(nxt, kcol + 1, D_KV),
                  ctx, ctx,
                  tab(t_cur), tab(t_cur), tab(t_prev), tab(t_prev), tab(t_next), tab(t_next),
                  pl.BlockSpec((1, ATTN_HEADS), lambda b, n: (0, 0))],
        out_specs=pl.BlockSpec((BLOCK, D_ATTN), lambda b, n: (b * nb + n, 0)),
        out_shape=jax.ShapeDtypeStruct((DEC_BATCH * DEC_SEQ, D_ATTN), jnp.float32),
        compiler_params=pltpu.CompilerParams(dimension_semantics=("parallel", "parallel"),
                                             vmem_limit_bytes=VMEM_LIMIT),
        name="window_attention",
    )(u, u, u, u, u, u, u, k_ctx, v_ctx, cos, sin, cos, sin, cos, sin, sink.reshape(1, ATTN_HEADS))


CONV_ROWS = 256
CONV_HALO = 16


def _glu(u):
    return u[:, :D_CONV] * jax.nn.sigmoid(u[:, D_CONV:])


def _conv_body(up_ref, uc_ref, un_ref, w_ref, b_ref, g_ref, beta_ref, o_ref, ext):
    i = pl.program_id(0)
    blocks_per_sample = DEC_SEQ // CONV_ROWS
    j = (i - N_PROMPT // CONV_ROWS) % blocks_per_sample
    in_sample = i >= N_PROMPT // CONV_ROWS
    has_prev = in_sample & (j > 0)
    has_next = in_sample & (j < blocks_per_sample - 1)
    halo = CONV_HALO
    ext[0:halo] = jnp.where(has_prev, _glu(up_ref[CONV_ROWS - halo:, :]), 0.0)
    ext[halo:halo + CONV_ROWS] = _glu(uc_ref[...])
    ext[halo + CONV_ROWS:] = jnp.where(has_next, _glu(un_ref[0:halo, :]), 0.0)
    acc = jnp.zeros((CONV_ROWS, D_CONV), jnp.float32)
    for tap in range(CONV_WIDTH):
        start = halo - CONV_WIDTH // 2 + tap
        acc = acc + ext[start:start + CONV_ROWS, :] * w_ref[tap:tap + 1, :]
    y = acc + b_ref[...]
    mu = jnp.mean(y, axis=-1, keepdims=True)
    var = jnp.mean(jnp.square(y - mu), axis=-1, keepdims=True)
    y = (y - mu) * lax.rsqrt(var + LN_EPS) * g_ref[...] + beta_ref[...]
    o_ref[...] = y * jax.nn.sigmoid(y)


def conformer_conv_pallas(u, w_dw, b_dw, ln_g, ln_b):
    n = u.shape[0]
    nblk = n // CONV_ROWS
    vec = lambda a: a.reshape(1, D_CONV)
    row = lambda f: pl.BlockSpec((CONV_ROWS, 2 * D_CONV), lambda i: (f(i), 0))
    one = pl.BlockSpec((1, D_CONV), lambda i: (0, 0))
    return pl.pallas_call(
        _conv_body,
        grid=(nblk,),
        in_specs=[row(lambda i: jnp.maximum(i - 1, 0)), row(lambda i: i), row(lambda i: jnp.minimum(i + 1, nblk - 1)),
                  pl.BlockSpec((CONV_WIDTH, D_CONV), lambda i: (0, 0)), one, one, one],
        out_specs=pl.BlockSpec((CONV_ROWS, D_CONV), lambda i: (i, 0)),
        out_shape=jax.ShapeDtypeStruct((n, D_CONV), jnp.float32),
        scratch_shapes=[pltpu.VMEM((CONV_ROWS + 2 * CONV_HALO, D_CONV), jnp.float32)],
        compiler_params=pltpu.CompilerParams(dimension_semantics=("parallel",), vmem_limit_bytes=VMEM_LIMIT),
        name="conformer_conv",
    )(u, u, u, w_dw, vec(b_dw), vec(ln_g), vec(ln_b))


PREP_ROWS = 256
PREP_HALO = 8


def _softplus(z):
    return jnp.maximum(z, 0.0) + jnp.log1p(jnp.exp(-jnp.abs(z)))


def _rwkv_prep_body(up_ref, uc_ref, un_ref, sw_ref, kk_ref, ka_ref, w0_ref, w2_ref, a0_ref, a2_ref, g2_ref,
                    r_ref, kh_ref, v_ref, wf_ref, kaf_ref, ktf_ref, wb_ref, kab_ref, ktb_ref, g_ref, ext):
    i = pl.program_id(0)
    blocks_per_sample = DEC_SEQ // PREP_ROWS
    j = (i - N_PROMPT // PREP_ROWS) % blocks_per_sample
    in_sample = i >= N_PROMPT // PREP_ROWS
    has_prev = in_sample & (j > 0)
    has_next = in_sample & (j < blocks_per_sample - 1)
    halo = PREP_HALO
    ext[0:halo] = jnp.where(has_prev, up_ref[PREP_ROWS - halo:, :], 0.0)
    ext[halo:halo + PREP_ROWS] = uc_ref[...]
    ext[halo + PREP_ROWS:] = jnp.where(has_next, un_ref[0:halo, :], 0.0)
    u = jnp.zeros((PREP_ROWS, D_RWKV_IN), jnp.float32)
    for tap in range(SHIFT_WIDTH):
        start = halo - SHIFT_WIDTH // 2 + tap
        u = u + ext[start:start + PREP_ROWS, :] * sw_ref[tap:tap + 1, :]
    d = D_RWKV
    r, k, v = u[:, :d], u[:, d:2 * d], u[:, 2 * d:3 * d]
    o = 3 * d
    wl = (u[:, o:o + LORA_W], u[:, o + LORA_W:o + 2 * LORA_W])
    o += 2 * LORA_W
    al = (u[:, o:o + LORA_A], u[:, o + LORA_A:o + 2 * LORA_A])
    gl = u[:, o + 2 * LORA_A:]
    r_ref[...] = r
    v_ref[...] = v
    kk = k * kk_ref[...]
    n = RWKV_HEAD_DIM
    kh = jnp.concatenate(
        [kk[:, h * n:(h + 1) * n] / jnp.maximum(jnp.sqrt(jnp.sum(jnp.square(kk[:, h * n:(h + 1) * n]), axis=1, keepdims=True)), 1e-12)
         for h in range(RWKV_HEADS)], axis=1)
    kh_ref[...] = kh
    bf = jnp.bfloat16
    outs = ((wf_ref, kaf_ref, ktf_ref), (wb_ref, kab_ref, ktb_ref))
    for idx in range(2):
        lw = jnp.dot(jnp.tanh(wl[idx]).astype(bf), w2_ref[idx], preferred_element_type=jnp.float32)
        logw = -_softplus(-(w0_ref[idx:idx + 1, :] + lw)) - 0.5
        a = jax.nn.sigmoid(a0_ref[idx:idx + 1, :] + jnp.dot(al[idx].astype(bf), a2_ref[idx], preferred_element_type=jnp.float32))
        w_o, ka_o, kt_o = outs[idx]
        w_o[...] = jnp.exp(-jnp.exp(logw))
        ka_o[...] = kh * a
        kt_o[...] = k * (1 + (a - 1) * ka_ref[...])
    g_ref[...] = jnp.dot(jax.nn.sigmoid(gl).astype(bf), g2_ref[...], preferred_element_type=jnp.float32)


def rwkv_prep_pallas(u_r, shift_w, k_k, k_a, w0, w2, a0, a2, g2):
    n = u_r.shape[0]
    nblk = n // PREP_ROWS
    bf = jnp.bfloat16
    row = lambda f: pl.BlockSpec((PREP_ROWS, D_RWKV_IN), lambda i: (f(i), 0))
    full = lambda a: pl.BlockSpec(a.shape, lambda i: (0,) * a.ndim)
    params = [shift_w, k_k.reshape(1, D_RWKV), k_a.reshape(1, D_RWKV), w0, w2.astype(bf), a0, a2.astype(bf), g2.astype(bf)]
    out = jax.ShapeDtypeStruct((n, D_RWKV), jnp.float32)
    return pl.pallas_call(
        _rwkv_prep_body,
        grid=(nblk,),
        in_specs=[row(lambda i: jnp.maximum(i - 1, 0)), row(lambda i: i), row(lambda i: jnp.minimum(i + 1, nblk - 1))]
                 + [full(a) for a in params],
        out_specs=[pl.BlockSpec((PREP_ROWS, D_RWKV), lambda i: (i, 0))] * 10,
        out_shape=[out] * 10,
        scratch_shapes=[pltpu.VMEM((PREP_ROWS + 2 * PREP_HALO, D_RWKV_IN), jnp.float32)],
        compiler_params=pltpu.CompilerParams(dimension_semantics=("parallel",), vmem_limit_bytes=VMEM_LIMIT),
        name="rwkv_prep",
    )(u_r, u_r, u_r, *params)


def _rwkv_post_body(x_ref, gate_ref, a_ref, yf_ref, yb_ref, r_ref, v_ref, ktf_ref, ktb_ref, g_ref,
                    rk_ref, gng_ref, gnb_ref, wa_ref, wb_ref, o_ref):
    n = RWKV_HEAD_DIM
    y = yf_ref[...] + yb_ref[...]
    rkk = r_ref[...] * 0.5 * (ktf_ref[...] + ktb_ref[...]) * rk_ref[...]
    v = v_ref[...]
    parts = []
    for h in range(RWKV_HEADS):
        hs = slice(h * n, (h + 1) * n)
        yh = y[:, hs]
        mu = jnp.mean(yh, axis=1, keepdims=True)
        var = jnp.mean(jnp.square(yh - mu), axis=1, keepdims=True)
        yn = (yh - mu) * lax.rsqrt(var + GN_EPS) * gng_ref[:, hs] + gnb_ref[:, hs]
        parts.append(yn + jnp.sum(rkk[:, hs], axis=1, keepdims=True) * v[:, hs])
    b_out = jnp.concatenate(parts, axis=1) * g_ref[...]
    bf = jnp.bfloat16
    acc = jnp.dot(a_ref[...].astype(bf), wa_ref[...], preferred_element_type=jnp.float32)
    acc = acc + jnp.dot(b_out.astype(bf), wb_ref[...], preferred_element_type=jnp.float32)
    o_ref[...] = x_ref[...] + gate_ref[0] * acc


def rwkv_post_pallas(x, gate, a_out, y_f, y_b, r, v, kt_f, kt_b, g, r_k, gn_g, gn_b, w_out, tm=512):
    n, d = x.shape
    bf = jnp.bfloat16
    rowmap = functools.partial(_mod_row, rows_per_block=tm)
    tok = lambda width: pl.BlockSpec((tm, width), lambda i: (i, 0))
    vec = pl.BlockSpec((1, D_RWKV), lambda i: (0, 0))
    wsp = pl.BlockSpec((D_RWKV, d), lambda i: (0, 0))
    return pl.pallas_call(
        _rwkv_post_body,
        grid=(n // tm,),
        in_specs=[tok(d), pl.BlockSpec((1, 1, d), lambda i: (rowmap(i), 0, 0))] + [tok(D_RWKV)] * 8 + [vec] * 3 + [wsp] * 2,
        out_specs=tok(d),
        out_shape=jax.ShapeDtypeStruct((n, d), jnp.float32),
        compiler_params=pltpu.CompilerParams(dimension_semantics=("parallel",), vmem_limit_bytes=VMEM_LIMIT),
        name="rwkv_post",
    )(x, gate, a_out, y_f, y_b, r, v, kt_f, kt_b, g, r_k.reshape(1, D_RWKV), gn_g.reshape(1, D_RWKV),
      gn_b.reshape(1, D_RWKV), w_out[:D_ATTN].astype(bf), w_out[D_ATTN:].astype(bf))


HEAD_PAIRS = RWKV_HEADS // 2
SCAN_STEPS = 128


def _seg_sums(x, seg_a):
    return (jnp.sum(jnp.where(seg_a, x, 0.0), axis=1, keepdims=True),
            jnp.sum(jnp.where(seg_a, 0.0, x), axis=1, keepdims=True))


def _rwkv_scan_body(r_f, kh_f, v_f, w_f, ka_f, kt_f, r_b, kh_b, v_b, w_b, ka_b, kt_b, s0_ref,
                    yf_ref, yb_ref, sfin_ref, s_scr, yacc, vt_scr):
    tb = pl.program_id(1)
    n = RWKV_HEAD_DIM
    tt = SCAN_STEPS

    @pl.when(tb == 0)
    def _():
        s_scr[...] = s0_ref[0]

    ins = ((r_f, kh_f, v_f, w_f, ka_f, kt_f), (r_b, kh_b, v_b, w_b, ka_b, kt_b))
    for d in range(2):
        for p in range(HEAD_PAIRS):
            vt_scr[d, p] = ins[d][2][:, p * LANES:(p + 1) * LANES].T

    lane = lax.broadcasted_iota(jnp.int32, (n, LANES), 1)
    seg_a = lane < n
    lane2 = lax.broadcasted_iota(jnp.int32, (LANES, LANES), 1)
    groups = tt // SUBLANES

    def group(g, carry):
        bases = (pl.multiple_of(g * SUBLANES, SUBLANES), pl.multiple_of((groups - 1 - g) * SUBLANES, SUBLANES))
        rows = [[ref[pl.ds(bases[d], SUBLANES), :] for ref in (ins[d][0], ins[d][1], ins[d][3], ins[d][4], ins[d][5])]
                for d in range(2)]
        pairs = range(HEAD_PAIRS)
        for step in range(SUBLANES):
            for d in range(2):
                ii = step if d == 0 else SUBLANES - 1 - step
                at_t = lane2 == bases[d] + ii
                vec = [[x[ii:ii + 1, p * LANES:(p + 1) * LANES] for x in rows[d]] for p in pairs]
                s_old = [s_scr[d, p] for p in pairs]
                sk = [_seg_sums(s_old[p] * vec[p][1], seg_a) for p in pairs]
                vcol = [jnp.sum(jnp.where(at_t, vt_scr[d, p], 0.0), axis=1, keepdims=True) for p in pairs]
                ysum = []
                for p in pairs:
                    r_t, _, w_t, ka_t, kt_t = vec[p]
                    vc = jnp.where(seg_a, vcol[p][:n], vcol[p][n:])
                    s = s_old[p] * w_t - jnp.where(seg_a, *sk[p]) * ka_t + vc * kt_t
                    s_scr[d, p] = s
                    ysum.append(_seg_sums(s * r_t, seg_a))
                for p in pairs:
                    ya, yb = ysum[p]
                    y = jnp.concatenate([jnp.broadcast_to(ya, (n, LANES)), jnp.broadcast_to(yb, (n, LANES))], axis=0)
                    yacc[d, p] = jnp.where(at_t, y, yacc[d, p])
        return carry

    lax.fori_loop(0, groups, group, 0)
    for p in range(HEAD_PAIRS):
        yf_ref[:, p * LANES:(p + 1) * LANES] = yacc[0, p].T
        yb_ref[:, p * LANES:(p + 1) * LANES] = yacc[1, p].T

    @pl.when(tb == pl.num_programs(1) - 1)
    def _():
        sfin_ref[0] = s_scr[...]


def rwkv_scan_pallas(r, kh, v, w_f, ka_f, kt_f, w_b, ka_b, kt_b, s0p, row0, b, t):
    dm = r.shape[1]
    tt = SCAN_STEPS
    nt = t // tt
    blk0 = row0 // tt
    fwd_in = pl.BlockSpec((tt, dm), lambda i, j: (blk0 + i * nt + j, 0))
    bwd_in = pl.BlockSpec((tt, dm), lambda i, j: (blk0 + i * nt + nt - 1 - j, 0))
    fwd_out = pl.BlockSpec((tt, dm), lambda i, j: (i * nt + j, 0))
    bwd_out = pl.BlockSpec((tt, dm), lambda i, j: (i * nt + nt - 1 - j, 0))
    st = pl.BlockSpec((1, 2, HEAD_PAIRS, RWKV_HEAD_DIM, LANES), lambda i, j: (i, 0, 0, 0, 0))
    return pl.pallas_call(
        _rwkv_scan_body,
        grid=(b, nt),
        in_specs=[fwd_in] * 6 + [bwd_in] * 6 + [st],
        out_specs=[fwd_out, bwd_out, st],
        out_shape=[jax.ShapeDtypeStruct((b * t, dm), jnp.float32), jax.ShapeDtypeStruct((b * t, dm), jnp.float32),
                   jax.ShapeDtypeStruct(s0p.shape, jnp.float32)],
        scratch_shapes=[pltpu.VMEM((2, HEAD_PAIRS, RWKV_HEAD_DIM, LANES), jnp.float32),
                        pltpu.VMEM((2, HEAD_PAIRS, LANES, LANES), jnp.float32),
                        pltpu.VMEM((2, HEAD_PAIRS, LANES, tt), jnp.float32)],
        compiler_params=pltpu.CompilerParams(dimension_semantics=("parallel", "arbitrary"),
                                             vmem_limit_bytes=VMEM_LIMIT),
        name="rwkv_scan",
    )(r, kh, v, w_f, ka_f, kt_f, r, kh, v, w_b, ka_b, kt_b, s0p)


def pack_state(s):
    b = s.shape[0]
    s = s.reshape(b, 2, HEAD_PAIRS, 2, RWKV_HEAD_DIM, RWKV_HEAD_DIM)
    return s.transpose(0, 1, 2, 4, 3, 5).reshape(b, 2, HEAD_PAIRS, RWKV_HEAD_DIM, LANES)


def unpack_state(sp):
    b = sp.shape[0]
    s = sp.reshape(b, 2, HEAD_PAIRS, RWKV_HEAD_DIM, 2, RWKV_HEAD_DIM)
    return s.transpose(0, 1, 2, 4, 3, 5).reshape(b, 2, RWKV_HEADS, RWKV_HEAD_DIM, RWKV_HEAD_DIM)


def _top_values(s, n):
    vals = []
    cur = s
    for a in range(n):
        m = jnp.max(cur, axis=0, keepdims=True)
        vals.append(m)
        if a + 1 < n:
            cur = jnp.where(cur == m, -jnp.inf, cur)
    return vals


def _peer_route_body(x_ref, g_ref, shift_ref, scale_ref, wq_ref, k1_ref, k2_ref,
                     hb_ref, th_ref, e1_ref, s2_ref, e2_ref, hb_s):
    h = pl.program_id(1)

    @pl.when(h == 0)
    def _():
        hn = _norm_mod(x_ref[...], g_ref[...], shift_ref[0], scale_ref[0])
        hb_s[...] = hn.astype(jnp.bfloat16)
        hb_ref[...] = hn.astype(jnp.bfloat16)

    half = D_QUERY // 2
    q = jnp.dot(hb_s[...], wq_ref[...], preferred_element_type=jnp.float32).astype(jnp.bfloat16)
    nt = (((1,), (1,)), ((), ()))
    s1 = lax.dot_general(k1_ref[0], q[:, :half], nt, preferred_element_type=jnp.float32)
    s2 = lax.dot_general(k2_ref[0], q[:, half:], nt, preferred_element_type=jnp.float32)
    k = PEER_TOPK
    v1 = _top_values(s1, k)
    v2 = _top_values(s2, k)
    v2s = jnp.concatenate(v2, axis=0)
    cand = [v1[0] + v2s]
    cand += [v1[a] + v2s[:8] for a in range(1, 8)]
    cand += [jnp.concatenate(v1[8:], axis=0) + v2[0]]
    tau = _top_values(jnp.concatenate(cand, axis=0), k)[-1]
    e2v = jnp.exp(v2s - v2[0])
    z = jnp.zeros_like(tau)
    theta = []
    for a in range(k):
        sel = (v1[a] + v2s) >= tau
        theta.append(jnp.min(jnp.where(sel, v2s, jnp.inf), axis=0, keepdims=True))
        z = z + jnp.exp(v1[a] - v1[0]) * jnp.sum(jnp.where(sel, e2v, 0.0), axis=0, keepdims=True)
    th = jnp.full_like(s1, jnp.inf)
    for a in range(k):
        th = jnp.where(s1 == v1[a], theta[a], th)
    th_ref[0] = th
    e1_ref[0] = jnp.where(s1 >= v1[k - 1], jnp.exp(s1 - v1[0]) / z, 0.0)
    s2_ref[0] = s2
    e2_ref[0] = jnp.exp(s2 - v2[0])


def peer_route(x, g, shift, scale, wq_b, k1_b, k2_b, tb):
    n, d = x.shape
    nb = n // tb
    hq = PEER_HEADS
    row = functools.partial(_mod_row, rows_per_block=tb)
    route_shape = jax.ShapeDtypeStruct((hq, N_KEYS, n), jnp.float32)
    route_spec = pl.BlockSpec((1, N_KEYS, tb), lambda i, h: (h, 0, i))
    return pl.pallas_call(
        _peer_route_body,
        grid=(nb, hq),
        in_specs=[
            pl.BlockSpec((tb, d), lambda i, h: (i, 0)),
            pl.BlockSpec((1, d), lambda i, h: (0, 0)),
            pl.BlockSpec((1, 1, d), lambda i, h: (row(i), 0, 0)),
            pl.BlockSpec((1, 1, d), lambda i, h: (row(i), 0, 0)),
            pl.BlockSpec((d, D_QUERY), lambda i, h: (0, h)),
            pl.BlockSpec((1, N_KEYS, D_QUERY // 2), lambda i, h: (h, 0, 0)),
            pl.BlockSpec((1, N_KEYS, D_QUERY // 2), lambda i, h: (h, 0, 0)),
        ],
        out_specs=[pl.BlockSpec((tb, d), lambda i, h: (i, 0)), route_spec, route_spec, route_spec, route_spec],
        out_shape=[jax.ShapeDtypeStruct((n, d), jnp.bfloat16), route_shape, route_shape, route_shape, route_shape],
        scratch_shapes=[pltpu.VMEM((tb, d), jnp.bfloat16)],
        compiler_params=pltpu.CompilerParams(dimension_semantics=("parallel", "arbitrary"),
                                             vmem_limit_bytes=VMEM_LIMIT),
        name="peer_route",
    )(x, g, shift, scale, wq_b, k1_b, k2_b)


def _gelu(x):
    return 0.5 * x * (1.0 + lax.erf(x * np.float32(1.0 / np.sqrt(2.0))))


def _peer_expert_body(x_ref, gate_ref, hb_ref, th_ref, e1_ref, s2_ref, e2_ref, u_ref, vt_ref,
                      o_ref, acc_s, g_s, *, te, tb):
    j = pl.program_id(1)

    @pl.when(j == 0)
    def _():
        acc_s[...] = jnp.zeros_like(acc_s)

    nt = (((1,), (1,)), ((), ()))
    act = _gelu(lax.dot_general(u_ref[...], hb_ref[...], nt, preferred_element_type=jnp.float32))
    lanes = 128
    keys_per_tile = te // N_KEYS
    i0 = pl.multiple_of(j * keys_per_tile, keys_per_tile)
    for c in range(tb // lanes):
        cs = slice(c * lanes, (c + 1) * lanes)
        th = [th_ref[h, pl.ds(i0, keys_per_tile), cs] for h in range(PEER_HEADS)]
        e1 = [e1_ref[h, pl.ds(i0, keys_per_tile), cs] for h in range(PEER_HEADS)]
        for ii in range(keys_per_tile):
            w = jnp.zeros((N_KEYS, lanes), jnp.float32)
            for h in range(PEER_HEADS):
                keep = s2_ref[h, :, cs] >= th[h][ii:ii + 1]
                w = w + e1[h][ii:ii + 1] * jnp.where(keep, e2_ref[h, :, cs], 0.0)
            rs = slice(ii * N_KEYS, (ii + 1) * N_KEYS)
            g_s[rs, cs] = (w * act[rs, cs]).astype(jnp.bfloat16)
    acc_s[...] += jnp.dot(vt_ref[...], g_s[...], preferred_element_type=jnp.float32)

    @pl.when(j == pl.num_programs(1) - 1)
    def _():
        o_ref[...] = x_ref[...] + gate_ref[0] * acc_s[...].T


def peer_experts(x, gate, hb, th, e1, s2, e2, u_b, vt_b, tb, te):
    n, d = x.shape
    nb = n // tb
    ne = N_EXPERTS // te
    row = functools.partial(_mod_row, rows_per_block=tb)
    route_spec = pl.BlockSpec((PEER_HEADS, N_KEYS, tb), lambda i, j: (0, 0, i))
    return pl.pallas_call(
        functools.partial(_peer_expert_body, te=te, tb=tb),
        grid=(nb, ne),
        in_specs=[
            pl.BlockSpec((tb, d), lambda i, j: (i, 0)),
            pl.BlockSpec((1, 1, d), lambda i, j: (row(i), 0, 0)),
            pl.BlockSpec((tb, d), lambda i, j: (i, 0)),
            route_spec, route_spec, route_spec, route_spec,
            pl.BlockSpec((te, d), lambda i, j: (j, 0)),
            pl.BlockSpec((d, te), lambda i, j: (0, j)),
        ],
        out_specs=pl.BlockSpec((tb, d), lambda i, j: (i, 0)),
        out_shape=jax.ShapeDtypeStruct((n, d), jnp.float32),
        scratch_shapes=[pltpu.VMEM((d, tb), jnp.float32), pltpu.VMEM((te, tb), jnp.bfloat16)],
        compiler_params=pltpu.CompilerParams(dimension_semantics=("parallel", "arbitrary"),
                                             vmem_limit_bytes=VMEM_LIMIT),
        name="peer_experts",
    )(x, gate, hb, th, e1, s2, e2, u_b, vt_b)


def peer_layer(x, g, shift, scale, gate, w_q, keys, u_tab, v_tab, tb=512, te=1024):
    bf = jnp.bfloat16
    hb, th, e1, s2, e2 = peer_route(x, g, shift, scale, w_q.astype(bf), keys[0].astype(bf), keys[1].astype(bf), tb)
    return peer_experts(x, gate, hb, th, e1, s2, e2, u_tab.astype(bf), v_tab.T.astype(bf), tb, te)


def _final_rms_body(x_ref, g_ref, o_ref):
    x = x_ref[...]
    o_ref[...] = x * lax.rsqrt(jnp.mean(x * x, axis=-1, keepdims=True) + RMS_EPS) * g_ref[...]


def final_rmsnorm(x, g):
    n, d = x.shape
    rows = 512
    return pl.pallas_call(
        _final_rms_body,
        grid=(n // rows,),
        in_specs=[pl.BlockSpec((rows, d), lambda i: (i, 0)), pl.BlockSpec((1, d), lambda i: (0, 0))],
        out_specs=pl.BlockSpec((rows, d), lambda i: (i, 0)),
        out_shape=jax.ShapeDtypeStruct((n, d), x.dtype),
        compiler_params=pltpu.CompilerParams(dimension_semantics=("parallel",)),
        name="final_rmsnorm",
    )(x, g.reshape(1, d))


def kernel(x_prompt, x_sample, cache_k, cache_v, state_rwkv, c, c_ctx, mod_w, mod_b, norm_g,
           ab_w_in, ab_w_out, ab_sink, ab_shift_w, ab_k_k, ab_k_a, ab_r_k, ab_w0, ab_w2,
           ab_a0, ab_a2, ab_g2, ab_gn_g, ab_gn_b, cv_w_pw1, cv_b_pw1, cv_w_dw, cv_b_dw,
           cv_ln_g, cv_ln_b, cv_w_pw2, cv_b_pw2, peer_w_q, peer_keys, peer_u, peer_v, final_g):
    d = D_MODEL
    n_sample = DEC_BATCH * DEC_SEQ
    x = jnp.concatenate([x_prompt.reshape(N_PROMPT, d), x_sample.reshape(n_sample, d)], axis=0)
    cond = jnp.concatenate([c_ctx[None, :], c], axis=0)
    new_k, new_v, new_s = [], [], []
    for layer in range(DEPTH):
        i = layer // 2
        mod = fused_proj(cond, mod_w[layer], silu=True, bias=mod_b[layer], name="modulation")
        s1, c1, g1, s2, c2, g2 = (mod[:, j * d:(j + 1) * d].reshape(-1, 1, d) for j in range(6))
        norm_mod = (norm_g[layer, 0][None, :], s1, c1)
        if layer % 2 == 0:
            u_a, u_r = fused_proj(x, ab_w_in[i], norm_mod=norm_mod, splits=(D_ATTN_IN, D_RWKV_IN), name="in_proj")
            a_out = jnp.concatenate([
                context_attention_pallas(u_a, ab_sink[i]),
                window_attention_pallas(u_a, cache_k[:, i].reshape(DEC_BATCH, PAST_LEN, D_KV),
                                        cache_v[:, i].reshape(DEC_BATCH, PAST_LEN, D_KV), ab_sink[i])], axis=0)
            r, kh, v, w_f, ka_f, kt_f, w_b, ka_b, kt_b, g = rwkv_prep_pallas(
                u_r, ab_shift_w[i], ab_k_k[i], ab_k_a[i], ab_w0[i], ab_w2[i], ab_a0[i], ab_a2[i], ab_g2[i])
            scan_in = (r, kh, v, w_f, ka_f, kt_f, w_b, ka_b, kt_b)
            zeros = jnp.zeros((BATCH, 2, HEAD_PAIRS, RWKV_HEAD_DIM, LANES), jnp.float32)
            yf_p, yb_p, s_p = rwkv_scan_pallas(*scan_in, zeros, 0, BATCH, SEQ)
            yf_s, yb_s, _ = rwkv_scan_pallas(*scan_in, pack_state(state_rwkv[:, i]), N_PROMPT, DEC_BATCH, DEC_SEQ)
            y_f = jnp.concatenate([yf_p, yf_s], axis=0)
            y_b = jnp.concatenate([yb_p, yb_s], axis=0)
            x = rwkv_post_pallas(x, g1, a_out, y_f, y_b, r, v, kt_f, kt_b, g,
                                 ab_r_k[i], ab_gn_g[i], ab_gn_b[i], ab_w_out[i])
            new_k.append(u_a[:N_PROMPT, D_ATTN:D_ATTN + D_KV].reshape(BATCH, SEQ, KV_HEADS, HEAD_DIM))
            new_v.append(u_a[:N_PROMPT, D_ATTN + D_KV:].reshape(BATCH, SEQ, KV_HEADS, HEAD_DIM))
            new_s.append(unpack_state(s_p))
        else:
            u = fused_proj(x, cv_w_pw1[i], norm_mod=norm_mod, bias=cv_b_pw1[i], name="pw1")
            u = conformer_conv_pallas(u, cv_w_dw[i], cv_b_dw[i], cv_ln_g[i], cv_ln_b[i])
            x = fused_proj(u, cv_w_pw2[i], bias=cv_b_pw2[i], residual=(x, g1), name="pw2")
        x = peer_layer(x, norm_g[layer, 1][None, :], s2, c2, g2,
                       peer_w_q[layer], peer_keys[layer], peer_u[layer], peer_v[layer])
    y = final_rmsnorm(x, final_g)
    y_prompt = y[:N_PROMPT].reshape(BATCH, SEQ, d)
    y_sample = y[N_PROMPT:].reshape(DEC_BATCH, DEC_SEQ, d)
    return (y_prompt, y_sample, jnp.stack(new_k, axis=1), jnp.stack(new_v, axis=1), jnp.stack(new_s, axis=1))
```

```python
import functools
import jax
import jax.numpy as jnp
from jax import lax
import numpy as np
from jax.experimental import pallas as pl
from jax.experimental.pallas import tpu as pltpu

D_MODEL = 1024
BATCH = 32
SEQ = 256
DEPTH = 2
DEC_BATCH = 4
DEC_SEQ = 4096
PAST_LEN = 256

GRID_W = 64
BLOCK = 128
WINDOW = 128
ATTN_HEADS = 8
KV_HEADS = 2
GROUP = ATTN_HEADS // KV_HEADS
HEAD_DIM = 64
D_ATTN = ATTN_HEADS * HEAD_DIM
D_ATTN_IN = D_ATTN + 2 * KV_HEADS * HEAD_DIM
RWKV_HEADS = 8
RWKV_HEAD_DIM = 64
D_RWKV = RWKV_HEADS * RWKV_HEAD_DIM
LORA_W = 64
LORA_A = 64
LORA_G = 128
D_RWKV_IN = 3 * D_RWKV + 2 * LORA_W + 2 * LORA_A + LORA_G
D_IN_AB = D_ATTN_IN + D_RWKV_IN
SHIFT_WIDTH = 3
ROPE_BASE = 10000.0
D_CONV = D_MODEL
CONV_WIDTH = 31
PEER_HEADS = 8
N_KEYS = 128
N_EXPERTS = N_KEYS * N_KEYS
D_QUERY = 256
PEER_TOPK = 16
N_EVEN = (DEPTH + 1) // 2
N_ODD = DEPTH // 2
RMS_EPS = 1e-6
LN_EPS = 1e-5
GN_EPS = 64e-5
N_PROMPT = BATCH * SEQ
SAMPLE_LEN = DEC_SEQ
N_TOKENS = N_PROMPT + DEC_BATCH * DEC_SEQ
D_KV = KV_HEADS * HEAD_DIM
VMEM_LIMIT = 56 * 1024 * 1024
SUBLANES = 8
LANES = 128


def _mod_row(i, rows_per_block):
    tok = i * rows_per_block
    return jnp.where(tok < N_PROMPT, 0, 1 + (tok - N_PROMPT) // SAMPLE_LEN)


def _norm_mod(x, g, shift, scale):
    y = x * lax.rsqrt(jnp.mean(x * x, axis=-1, keepdims=True) + RMS_EPS)
    return (y * g) * (1 + scale) + shift


def _proj_body(*refs, prologue, has_bias, residual, splits):
    it = iter(refs)
    x_ref, w_ref = next(it), next(it)
    x = x_ref[...]
    if prologue == "norm_mod":
        g_ref, shift_ref, scale_ref = next(it), next(it), next(it)
        x = _norm_mod(x, g_ref[...], shift_ref[0], scale_ref[0])
    elif prologue == "silu":
        x = x * jax.nn.sigmoid(x)
    acc = jnp.dot(x.astype(jnp.bfloat16), w_ref[...], preferred_element_type=jnp.float32)
    if has_bias:
        acc = acc + next(it)[...]
    if residual:
        res_ref, gate_ref = next(it), next(it)
        acc = res_ref[...] + gate_ref[0] * acc
    start = 0
    for width in splits:
        next(it)[...] = acc[:, start:start + width]
        start += width


def fused_proj(x, w, *, norm_mod=None, silu=False, bias=None, residual=None, splits=None, tm=512, name="proj"):
    m, k = x.shape
    nout = w.shape[1]
    tm = min(tm, m)
    row = functools.partial(_mod_row, rows_per_block=tm)
    args = [x, w.astype(jnp.bfloat16)]
    specs = [pl.BlockSpec((tm, k), lambda i: (i, 0)), pl.BlockSpec((k, nout), lambda i: (0, 0))]
    prologue = None
    if norm_mod is not None:
        prologue = "norm_mod"
        args += list(norm_mod)
        specs += [pl.BlockSpec((1, k), lambda i: (0, 0)),
                  pl.BlockSpec((1, 1, k), lambda i: (row(i), 0, 0)),
                  pl.BlockSpec((1, 1, k), lambda i: (row(i), 0, 0))]
    elif silu:
        prologue = "silu"
    if bias is not None:
        args.append(bias.reshape(1, nout))
        specs.append(pl.BlockSpec((1, nout), lambda i: (0, 0)))
    if residual is not None:
        args += list(residual)
        specs += [pl.BlockSpec((tm, nout), lambda i: (i, 0)), pl.BlockSpec((1, 1, nout), lambda i: (row(i), 0, 0))]
    widths = (nout,) if splits is None else tuple(splits)
    assert sum(widths) == nout
    outs = pl.pallas_call(
        functools.partial(_proj_body, prologue=prologue, has_bias=bias is not None, residual=residual is not None,
                          splits=widths),
        grid=(m // tm,),
        in_specs=specs,
        out_specs=[pl.BlockSpec((tm, wd), lambda i: (i, 0)) for wd in widths],
        out_shape=[jax.ShapeDtypeStruct((m, wd), jnp.float32) for wd in widths],
        compiler_params=pltpu.CompilerParams(dimension_semantics=("parallel",), vmem_limit_bytes=VMEM_LIMIT),
        name=name,
    )(*args)
    return outs[0] if splits is None else outs


def _attend(q, keys, values, masks, sink):
    nt = (((1,), (1,)), ((), ()))
    scale = HEAD_DIM ** -0.5
    ss = []
    for k, msk in zip(keys, masks):
        s = lax.dot_general(q, k, nt, preferred_element_type=jnp.float32) * scale
        ss.append(s if msk is None else jnp.where(msk, s, -jnp.inf))
    m = sink
    for s in ss:
        m = jnp.maximum(m, jnp.max(s, axis=1, keepdims=True))
    ps = [jnp.exp(s - m) for s in ss]
    denom = jnp.exp(sink - m)
    for p in ps:
        denom = denom + jnp.sum(p, axis=1, keepdims=True)
    out = None
    for p, v in zip(ps, values):
        o = jnp.dot((p / denom).astype(jnp.bfloat16), v, preferred_element_type=jnp.float32)
        out = o if out is None else out + o
    return out


def _ctx_attn_body(q_ref, k_ref, v_ref, sink_ref, o_ref):
    bf = jnp.bfloat16
    q = q_ref[...].astype(bf)
    k = k_ref[...].astype(bf)
    v = v_ref[...].astype(bf)
    for h in range(ATTN_HEADS):
        kv = h // GROUP
        ks = slice(kv * HEAD_DIM, (kv + 1) * HEAD_DIM)
        hs = slice(h * HEAD_DIM, (h + 1) * HEAD_DIM)
        o_ref[:, hs] = _attend(q[:, hs], [k[:, ks]], [v[:, ks]], [None], sink_ref[0:1, h:h + 1])


def context_attention_pallas(u, sink):
    kcol = D_ATTN // D_KV
    return pl.pallas_call(
        _ctx_attn_body,
        grid=(BATCH,),
        in_specs=[pl.BlockSpec((SEQ, D_ATTN), lambda i: (i, 0)),
                  pl.BlockSpec((SEQ, D_KV), lambda i: (i, kcol)),
                  pl.BlockSpec((SEQ, D_KV), lambda i: (i, kcol + 1)),
                  pl.BlockSpec((1, ATTN_HEADS), lambda i: (0, 0))],
        out_specs=pl.BlockSpec((SEQ, D_ATTN), lambda i: (i, 0)),
        out_shape=jax.ShapeDtypeStruct((N_PROMPT, D_ATTN), jnp.float32),
        compiler_params=pltpu.CompilerParams(dimension_semantics=("parallel",), vmem_limit_bytes=VMEM_LIMIT),
        name="context_attention",
    )(u, u, u, sink.reshape(1, ATTN_HEADS))


def rope_tables():
    t = np.arange(DEC_SEQ)
    quarter = HEAD_DIM // 4
    freqs = jnp.asarray(ROPE_BASE, jnp.float32) ** (-jnp.arange(quarter, dtype=jnp.float32) / quarter)
    row = jnp.asarray(t // GRID_W, jnp.float32)[:, None] * freqs[None, :]
    col = jnp.asarray(t % GRID_W, jnp.float32)[:, None] * freqs[None, :]
    cos = jnp.concatenate([jnp.cos(row), jnp.cos(row), jnp.cos(col), jnp.cos(col)], axis=1)
    sin = jnp.concatenate([-jnp.sin(row), jnp.sin(row), -jnp.sin(col), jnp.sin(col)], axis=1)
    return cos, sin


def _rope(x, cos, sin, first):
    quarter = HEAD_DIM // 4
    width = x.shape[1]
    partner = jnp.where(first, pltpu.roll(x, width - quarter, 1), pltpu.roll(x, quarter, 1))
    return x * cos + partner * sin


def _win_attn_body(q_ref, kp_ref, kc_ref, kn_ref, vp_ref, vc_ref, vn_ref, kx_ref, vx_ref,
                   cq_ref, sq_ref, cp_ref, sp_ref, cn_ref, sn_ref, sink_ref, o_ref):
    bf = jnp.bfloat16
    n = pl.program_id(1)
    nb = pl.num_programs(1)
    quarter = HEAD_DIM // 4
    lane_q = lax.broadcasted_iota(jnp.int32, (BLOCK, D_ATTN), 1)
    lane_k = lax.broadcasted_iota(jnp.int32, (BLOCK, D_KV), 1)
    cq, sq = cq_ref[...], sq_ref[...]
    tile_q = lambda a: jnp.concatenate([a] * ATTN_HEADS, axis=1)
    tile_k = lambda a: jnp.concatenate([a] * KV_HEADS, axis=1)
    q = _rope(q_ref[...], tile_q(cq), tile_q(sq), (lane_q % (2 * quarter)) < quarter).astype(bf)
    first_k = (lane_k % (2 * quarter)) < quarter
    kp = _rope(kp_ref[...], tile_k(cp_ref[...]), tile_k(sp_ref[...]), first_k).astype(bf)
    kc = _rope(kc_ref[...], tile_k(cq), tile_k(sq), first_k).astype(bf)
    kn = _rope(kn_ref[...], tile_k(cn_ref[...]), tile_k(sn_ref[...]), first_k).astype(bf)
    vp, vc, vn = vp_ref[...].astype(bf), vc_ref[...].astype(bf), vn_ref[...].astype(bf)
    kx, vx = kx_ref[0].astype(bf), vx_ref[0].astype(bf)
    qi = lax.broadcasted_iota(jnp.int32, (BLOCK, BLOCK), 0)
    si = lax.broadcasted_iota(jnp.int32, (BLOCK, BLOCK), 1)
    m_prev = (si >= qi) & (n > 0)
    m_next = (si <= qi) & (n + 1 < nb)
    for h in range(ATTN_HEADS):
        kv = h // GROUP
        ks = slice(kv * HEAD_DIM, (kv + 1) * HEAD_DIM)
        hs = slice(h * HEAD_DIM, (h + 1) * HEAD_DIM)
        o_ref[:, hs] = _attend(q[:, hs], [kp[:, ks], kc[:, ks], kn[:, ks], kx[:, ks]],
                               [vp[:, ks], vc[:, ks], vn[:, ks], vx[:, ks]],
                               [m_prev, None, m_next, None], sink_ref[0:1, h:h + 1])


def window_attention_pallas(u, k_ctx, v_ctx, sink):
    nb = DEC_SEQ // BLOCK
    base = N_PROMPT // BLOCK
    kcol = D_ATTN // D_KV
    cos, sin = rope_tables()
    cur = lambda b, n: base + b * nb + n
    prev = lambda b, n: base + b * nb + jnp.maximum(n - 1, 0)
    nxt = lambda b, n: base + b * nb + jnp.minimum(n + 1, nb - 1)
    blk = lambda rowf, col, width: pl.BlockSpec((BLOCK, width), lambda b, n: (rowf(b, n), col))
    tab = lambda f: pl.BlockSpec((BLOCK, HEAD_DIM), lambda b, n: (f(n), 0))
    t_cur = lambda n: n
    t_prev = lambda n: jnp.maximum(n - 1, 0)
    t_next = lambda n: jnp.minimum(n + 1, nb - 1)
    ctx = pl.BlockSpec((1, PAST_LEN, D_KV), lambda b, n: (b, 0, 0))
    return pl.pallas_call(
        _win_attn_body,
        grid=(DEC_BATCH, nb),
        in_specs=[blk(cur, 0, D_ATTN),
                  blk(prev, kcol, D_KV), blk(cur, kcol, D_KV), blk(nxt, kcol, D_KV),
                  blk(prev, kcol + 1, D_KV), blk(cur, kcol + 1, D_KV), blk(nxt, kcol + 1, D_KV),
                  ctx, ctx,
                  tab(t_cur), tab(t_cur), tab(t_prev), tab(t_prev), tab(t_next), tab(t_next),
                  pl.BlockSpec((1, ATTN_HEADS), lambda b, n: (0, 0))],
        out_specs=pl.BlockSpec((BLOCK, D_ATTN), lambda b, n: (b * nb + n, 0)),
        out_shape=jax.ShapeDtypeStruct((DEC_BATCH * DEC_SEQ, D_ATTN), jnp.float32),
        compiler_params=pltpu.CompilerParams(dimension_semantics=("parallel", "parallel"),
                                             vmem_limit_bytes=VMEM_LIMIT),
        name="window_attention",
    )(u, u, u, u, u, u, u, k_ctx, v_ctx, cos, sin, cos, sin, cos, sin, sink.reshape(1, ATTN_HEADS))


CONV_ROWS = 256
CONV_HALO = 16


def _glu(u):
    return u[:, :D_CONV] * jax.nn.sigmoid(u[:, D_CONV:])


def _conv_body(up_ref, uc_ref, un_ref, w_ref, b_ref, g_ref, beta_ref, o_ref, ext):
    i = pl.program_id(0)
    blocks_per_sample = DEC_SEQ // CONV_ROWS
    j = (i - N_PROMPT // CONV_ROWS) % blocks_per_sample
    in_sample = i >= N_PROMPT // CONV_ROWS
    has_prev = in_sample & (j > 0)
    has_next = in_sample & (j < blocks_per_sample - 1)
    halo = CONV_HALO
    ext[0:halo] = jnp.where(has_prev, _glu(up_ref[CONV_ROWS - halo:, :]), 0.0)
    ext[halo:halo + CONV_ROWS] = _glu(uc_ref[...])
    ext[halo + CONV_ROWS:] = jnp.where(has_next, _glu(un_ref[0:halo, :]), 0.0)
    acc = jnp.zeros((CONV_ROWS, D_CONV), jnp.float32)
    for tap in range(CONV_WIDTH):
        start = halo - CONV_WIDTH // 2 + tap
        acc = acc + ext[start:start + CONV_ROWS, :] * w_ref[tap:tap + 1, :]
    y = acc + b_ref[...]
    mu = jnp.mean(y, axis=-1, keepdims=True)
    var = jnp.mean(jnp.square(y - mu), axis=-1, keepdims=True)
    y = (y - mu) * lax.rsqrt(var + LN_EPS) * g_ref[...] + beta_ref[...]
    o_ref[...] = y * jax.nn.sigmoid(y)


def conformer_conv_pallas(u, w_dw, b_dw, ln_g, ln_b):
    n = u.shape[0]
    nblk = n // CONV_ROWS
    vec = lambda a: a.reshape(1, D_CONV)
    row = lambda f: pl.BlockSpec((CONV_ROWS, 2 * D_CONV), lambda i: (f(i), 0))
    one = pl.BlockSpec((1, D_CONV), lambda i: (0, 0))
    return pl.pallas_call(
        _conv_body,
        grid=(nblk,),
        in_specs=[row(lambda i: jnp.maximum(i - 1, 0)), row(lambda i: i), row(lambda i: jnp.minimum(i + 1, nblk - 1)),
                  pl.BlockSpec((CONV_WIDTH, D_CONV), lambda i: (0, 0)), one, one, one],
        out_specs=pl.BlockSpec((CONV_ROWS, D_CONV), lambda i: (i, 0)),
        out_shape=jax.ShapeDtypeStruct((n, D_CONV), jnp.float32),
        scratch_shapes=[pltpu.VMEM((CONV_ROWS + 2 * CONV_HALO, D_CONV), jnp.float32)],
        compiler_params=pltpu.CompilerParams(dimension_semantics=("parallel",), vmem_limit_bytes=VMEM_LIMIT),
        name="conformer_conv",
    )(u, u, u, w_dw, vec(b_dw), vec(ln_g), vec(ln_b))


PREP_ROWS = 256
PREP_HALO = 8


def _softplus(z):
    return jnp.maximum(z, 0.0) + jnp.log1p(jnp.exp(-jnp.abs(z)))


def _rwkv_prep_body(up_ref, uc_ref, un_ref, sw_ref, kk_ref, ka_ref, w0_ref, w2_ref, a0_ref, a2_ref, g2_ref,
                    r_ref, kh_ref, v_ref, wf_ref, kaf_ref, ktf_ref, wb_ref, kab_ref, ktb_ref, g_ref, ext):
    i = pl.program_id(0)
    blocks_per_sample = DEC_SEQ // PREP_ROWS
    j = (i - N_PROMPT // PREP_ROWS) % blocks_per_sample
    in_sample = i >= N_PROMPT // PREP_ROWS
    has_prev = in_sample & (j > 0)
    has_next = in_sample & (j < blocks_per_sample - 1)
    halo = PREP_HALO
    ext[0:halo] = jnp.where(has_prev, up_ref[PREP_ROWS - halo:, :], 0.0)
    ext[halo:halo + PREP_ROWS] = uc_ref[...]
    ext[halo + PREP_ROWS:] = jnp.where(has_next, un_ref[0:halo, :], 0.0)
    u = jnp.zeros((PREP_ROWS, D_RWKV_IN), jnp.float32)
    for tap in range(SHIFT_WIDTH):
        start = halo - SHIFT_WIDTH // 2 + tap
        u = u + ext[start:start + PREP_ROWS, :] * sw_ref[tap:tap + 1, :]
    d = D_RWKV
    r, k, v = u[:, :d], u[:, d:2 * d], u[:, 2 * d:3 * d]
    o = 3 * d
    wl = (u[:, o:o + LORA_W], u[:, o + LORA_W:o + 2 * LORA_W])
    o += 2 * LORA_W
    al = (u[:, o:o + LORA_A], u[:, o + LORA_A:o + 2 * LORA_A])
    gl = u[:, o + 2 * LORA_A:]
    r_ref[...] = r
    v_ref[...] = v
    kk = k * kk_ref[...]
    n = RWKV_HEAD_DIM
    kh = jnp.concatenate(
        [kk[:, h * n:(h + 1) * n] / jnp.maximum(jnp.sqrt(jnp.sum(jnp.square(kk[:, h * n:(h + 1) * n]), axis=1, keepdims=True)), 1e-12)
         for h in range(RWKV_HEADS)], axis=1)
    kh_ref[...] = kh
    bf = jnp.bfloat16
    outs = ((wf_ref, kaf_ref, ktf_ref), (wb_ref, kab_ref, ktb_ref))
    for idx in range(2):
        lw = jnp.dot(jnp.tanh(wl[idx]).astype(bf), w2_ref[idx], preferred_element_type=jnp.float32)
        logw = -_softplus(-(w0_ref[idx:idx + 1, :] + lw)) - 0.5
        a = jax.nn.sigmoid(a0_ref[idx:idx + 1, :] + jnp.dot(al[idx].astype(bf), a2_ref[idx], preferred_element_type=jnp.float32))
        w_o, ka_o, kt_o = outs[idx]
        w_o[...] = jnp.exp(-jnp.exp(logw))
        ka_o[...] = kh * a
        kt_o[...] = k * (1 + (a - 1) * ka_ref[...])
    g_ref[...] = jnp.dot(jax.nn.sigmoid(gl).astype(bf), g2_ref[...], preferred_element_type=jnp.float32)


def rwkv_prep_pallas(u_r, shift_w, k_k, k_a, w0, w2, a0, a2, g2):
    n = u_r.shape[0]
    nblk = n // PREP_ROWS
    bf = jnp.bfloat16
    row = lambda f: pl.BlockSpec((PREP_ROWS, D_RWKV_IN), lambda i: (f(i), 0))
    full = lambda a: pl.BlockSpec(a.shape, lambda i: (0,) * a.ndim)
    params = [shift_w, k_k.reshape(1, D_RWKV), k_a.reshape(1, D_RWKV), w0, w2.astype(bf), a0, a2.astype(bf), g2.astype(bf)]
    out = jax.ShapeDtypeStruct((n, D_RWKV), jnp.float32)
    return pl.pallas_call(
        _rwkv_prep_body,
        grid=(nblk,),
        in_specs=[row(lambda i: jnp.maximum(i - 1, 0)), row(lambda i: i), row(lambda i: jnp.minimum(i + 1, nblk - 1))]
                 + [full(a) for a in params],
        out_specs=[pl.BlockSpec((PREP_ROWS, D_RWKV), lambda i: (i, 0))] * 10,
        out_shape=[out] * 10,
        scratch_shapes=[pltpu.VMEM((PREP_ROWS + 2 * PREP_HALO, D_RWKV_IN), jnp.float32)],
        compiler_params=pltpu.CompilerParams(dimension_semantics=("parallel",), vmem_limit_bytes=VMEM_LIMIT),
        name="rwkv_prep",
    )(u_r, u_r, u_r, *params)


def _rwkv_post_body(x_ref, gate_ref, a_ref, yf_ref, yb_ref, r_ref, v_ref, ktf_ref, ktb_ref, g_ref,
                    rk_ref, gng_ref, gnb_ref, wa_ref, wb_ref, o_ref):
    n = RWKV_HEAD_DIM
    y = yf_ref[...] + yb_ref[...]
    rkk = r_ref[...] * 0.5 * (ktf_ref[...] + ktb_ref[...]) * rk_ref[...]
    v = v_ref[...]
    parts = []
    for h in range(RWKV_HEADS):
        hs = slice(h * n, (h + 1) * n)
        yh = y[:, hs]
        mu = jnp.mean(yh, axis=1, keepdims=True)
        var = jnp.mean(jnp.square(yh - mu), axis=1, keepdims=True)
        yn = (yh - mu) * lax.rsqrt(var + GN_EPS) * gng_ref[:, hs] + gnb_ref[:, hs]
        parts.append(yn + jnp.sum(rkk[:, hs], axis=1, keepdims=True) * v[:, hs])
    b_out = jnp.concatenate(parts, axis=1) * g_ref[...]
    bf = jnp.bfloat16
    acc = jnp.dot(a_ref[...].astype(bf), wa_ref[...], preferred_element_type=jnp.float32)
    acc = acc + jnp.dot(b_out.astype(bf), wb_ref[...], preferred_element_type=jnp.float32)
    o_ref[...] = x_ref[...] + gate_ref[0] * acc


def rwkv_post_pallas(x, gate, a_out, y_f, y_b, r, v, kt_f, kt_b, g, r_k, gn_g, gn_b, w_out, tm=512):
    n, d = x.shape
    bf = jnp.bfloat16
    rowmap = functools.partial(_mod_row, rows_per_block=tm)
    tok = lambda width: pl.BlockSpec((tm, width), lambda i: (i, 0))
    vec = pl.BlockSpec((1, D_RWKV), lambda i: (0, 0))
    wsp = pl.BlockSpec((D_RWKV, d), lambda i: (0, 0))
    return pl.pallas_call(
        _rwkv_post_body,
        grid=(n // tm,),
        in_specs=[tok(d), pl.BlockSpec((1, 1, d), lambda i: (rowmap(i), 0, 0))] + [tok(D_RWKV)] * 8 + [vec] * 3 + [wsp] * 2,
        out_specs=tok(d),
        out_shape=jax.ShapeDtypeStruct((n, d), jnp.float32),
        compiler_params=pltpu.CompilerParams(dimension_semantics=("parallel",), vmem_limit_bytes=VMEM_LIMIT),
        name="rwkv_post",
    )(x, gate, a_out, y_f, y_b, r, v, kt_f, kt_b, g, r_k.reshape(1, D_RWKV), gn_g.reshape(1, D_RWKV),
      gn_b.reshape(1, D_RWKV), w_out[:D_ATTN].astype(bf), w_out[D_ATTN:].astype(bf))


HEAD_PAIRS = RWKV_HEADS // 2
SCAN_STEPS = 128


def _seg_sums(x, seg_a):
    return (jnp.sum(jnp.where(seg_a, x, 0.0), axis=1, keepdims=True),
            jnp.sum(jnp.where(seg_a, 0.0, x), axis=1, keepdims=True))


def _rwkv_scan_body(r_f, kh_f, v_f, w_f, ka_f, kt_f, r_b, kh_b, v_b, w_b, ka_b, kt_b, s0_ref,
                    yf_ref, yb_ref, sfin_ref, s_scr, yacc, vt_scr):
    tb = pl.program_id(1)
    n = RWKV_HEAD_DIM
    tt = SCAN_STEPS

    @pl.when(tb == 0)
    def _():
        s_scr[...] = s0_ref[0]

    ins = ((r_f, kh_f, v_f, w_f, ka_f, kt_f), (r_b, kh_b, v_b, w_b, ka_b, kt_b))
    bf = jnp.bfloat16
    for d in range(2):
        for p in range(HEAD_PAIRS):
            vt = ins[d][2][:, p * LANES:(p + 1) * LANES].T
            hi = vt.astype(bf)
            rest = vt - hi.astype(jnp.float32)
            mid = rest.astype(bf)
            lo = (rest - mid.astype(jnp.float32)).astype(bf)
            for piece, val in enumerate((hi, mid, lo)):
                vt_scr[d, p * LANES:(p + 1) * LANES, piece * tt:(piece + 1) * tt] = val
    yacc[...] = jnp.zeros_like(yacc)

    lane = lax.broadcasted_iota(jnp.int32, (n, LANES), 1)
    seg_a = lane < n
    lane2 = lax.broadcasted_iota(jnp.int32, (LANES, LANES), 1)
    sel_row = lax.broadcasted_iota(jnp.int32, (3 * tt, LANES), 0) % tt
    groups = tt // SUBLANES

    def group(g, carry):
        bases = (pl.multiple_of(g * SUBLANES, SUBLANES), pl.multiple_of((groups - 1 - g) * SUBLANES, SUBLANES))
        rows = [[ref[pl.ds(bases[d], SUBLANES), :] for ref in (ins[d][0], ins[d][1], ins[d][3], ins[d][4], ins[d][5])]
                for d in range(2)]
        pairs = range(HEAD_PAIRS)
        for step in range(SUBLANES):
            for d in range(2):
                ii = step if d == 0 else SUBLANES - 1 - step
                at_t = lane2 == bases[d] + ii
                vec = [[x[ii:ii + 1, p * LANES:(p + 1) * LANES] for x in rows[d]] for p in pairs]
                s_old = [s_scr[d, p] for p in pairs]
                sk = [_seg_sums(s_old[p] * vec[p][1], seg_a) for p in pairs]
                sel = jnp.where(sel_row == bases[d] + ii, 1.0, 0.0).astype(bf)
                vcol = jnp.dot(vt_scr[d], sel, preferred_element_type=jnp.float32)
                ysum = []
                for p in pairs:
                    r_t, _, w_t, ka_t, kt_t = vec[p]
                    vc = jnp.where(seg_a, vcol[p * LANES:p * LANES + n], vcol[p * LANES + n:(p + 1) * LANES])
                    s = s_old[p] * w_t - jnp.where(seg_a, *sk[p]) * ka_t + vc * kt_t
                    s_scr[d, p] = s
                    ysum.append(_seg_sums(s * r_t, seg_a))
                for p in pairs:
                    ya, yb = ysum[p]
                    y = jnp.concatenate([jnp.broadcast_to(ya, (n, LANES)), jnp.broadcast_to(yb, (n, LANES))], axis=0)
                    yacc[d, p] = jnp.where(at_t, y, yacc[d, p])
        return carry

    lax.fori_loop(0, groups, group, 0)
    for p in range(HEAD_PAIRS):
        yf_ref[:, p * LANES:(p + 1) * LANES] = yacc[0, p].T
        yb_ref[:, p * LANES:(p + 1) * LANES] = yacc[1, p].T

    @pl.when(tb == pl.num_programs(1) - 1)
    def _():
        sfin_ref[0] = s_scr[...]


def rwkv_scan_pallas(r, kh, v, w_f, ka_f, kt_f, w_b, ka_b, kt_b, s0p, row0, b, t):
    dm = r.shape[1]
    tt = SCAN_STEPS
    nt = t // tt
    blk0 = row0 // tt
    fwd_in = pl.BlockSpec((tt, dm), lambda i, j: (blk0 + i * nt + j, 0))
    bwd_in = pl.BlockSpec((tt, dm), lambda i, j: (blk0 + i * nt + nt - 1 - j, 0))
    fwd_out = pl.BlockSpec((tt, dm), lambda i, j: (i * nt + j, 0))
    bwd_out = pl.BlockSpec((tt, dm), lambda i, j: (i * nt + nt - 1 - j, 0))
    st = pl.BlockSpec((1, 2, HEAD_PAIRS, RWKV_HEAD_DIM, LANES), lambda i, j: (i, 0, 0, 0, 0))
    return pl.pallas_call(
        _rwkv_scan_body,
        grid=(b, nt),
        in_specs=[fwd_in] * 6 + [bwd_in] * 6 + [st],
        out_specs=[fwd_out, bwd_out, st],
        out_shape=[jax.ShapeDtypeStruct((b * t, dm), jnp.float32), jax.ShapeDtypeStruct((b * t, dm), jnp.float32),
                   jax.ShapeDtypeStruct(s0p.shape, jnp.float32)],
        scratch_shapes=[pltpu.VMEM((2, HEAD_PAIRS, RWKV_HEAD_DIM, LANES), jnp.float32),
                        pltpu.VMEM((2, HEAD_PAIRS, LANES, LANES), jnp.float32),
                        pltpu.VMEM((2, HEAD_PAIRS * LANES, 3 * tt), jnp.bfloat16)],
        compiler_params=pltpu.CompilerParams(dimension_semantics=("parallel", "arbitrary"),
                                             vmem_limit_bytes=VMEM_LIMIT),
        name="rwkv_scan",
    )(r, kh, v, w_f, ka_f, kt_f, r, kh, v, w_b, ka_b, kt_b, s0p)


def pack_state(s):
    b = s.shape[0]
    s = s.reshape(b, 2, HEAD_PAIRS, 2, RWKV_HEAD_DIM, RWKV_HEAD_DIM)
    return s.transpose(0, 1, 2, 4, 3, 5).reshape(b, 2, HEAD_PAIRS, RWKV_HEAD_DIM, LANES)


def unpack_state(sp):
    b = sp.shape[0]
    s = sp.reshape(b, 2, HEAD_PAIRS, RWKV_HEAD_DIM, 2, RWKV_HEAD_DIM)
    return s.transpose(0, 1, 2, 4, 3, 5).reshape(b, 2, RWKV_HEADS, RWKV_HEAD_DIM, RWKV_HEAD_DIM)


def _top_values(s, n):
    vals = []
    cur = s
    for a in range(n):
        m = jnp.max(cur, axis=0, keepdims=True)
        vals.append(m)
        if a + 1 < n:
            cur = jnp.where(cur == m, -jnp.inf, cur)
    return vals


def _peer_route_body(x_ref, g_ref, shift_ref, scale_ref, wq_ref, k1_ref, k2_ref,
                     hb_ref, th_ref, e1_ref, s2_ref, e2_ref, hb_s):
    h = pl.program_id(1)

    @pl.when(h == 0)
    def _():
        hn = _norm_mod(x_ref[...], g_ref[...], shift_ref[0], scale_ref[0])
        hb_s[...] = hn.astype(jnp.bfloat16)
        hb_ref[...] = hn.astype(jnp.bfloat16)

    half = D_QUERY // 2
    q = jnp.dot(hb_s[...], wq_ref[...], preferred_element_type=jnp.float32).astype(jnp.bfloat16)
    nt = (((1,), (1,)), ((), ()))
    s1 = lax.dot_general(k1_ref[0], q[:, :half], nt, preferred_element_type=jnp.float32)
    s2 = lax.dot_general(k2_ref[0], q[:, half:], nt, preferred_element_type=jnp.float32)
    k = PEER_TOPK
    v1 = _top_values(s1, k)
    v2 = _top_values(s2, k)
    v2s = jnp.concatenate(v2, axis=0)
    cand = [v1[0] + v2s]
    cand += [v1[a] + v2s[:8] for a in range(1, 8)]
    cand += [jnp.concatenate(v1[8:], axis=0) + v2[0]]
    tau = _top_values(jnp.concatenate(cand, axis=0), k)[-1]
    e2v = jnp.exp(v2s - v2[0])
    z = jnp.zeros_like(tau)
    theta = []
    for a in range(k):
        sel = (v1[a] + v2s) >= tau
        theta.append(jnp.min(jnp.where(sel, v2s, jnp.inf), axis=0, keepdims=True))
        z = z + jnp.exp(v1[a] - v1[0]) * jnp.sum(jnp.where(sel, e2v, 0.0), axis=0, keepdims=True)
    th = jnp.full_like(s1, jnp.inf)
    for a in range(k):
        th = jnp.where(s1 == v1[a], theta[a], th)
    th_ref[0] = th
    e1_ref[0] = jnp.where(s1 >= v1[k - 1], jnp.exp(s1 - v1[0]) / z, 0.0)
    s2_ref[0] = s2
    e2_ref[0] = jnp.exp(s2 - v2[0])


def peer_route(x, g, shift, scale, wq_b, k1_b, k2_b, tb):
    n, d = x.shape
    nb = n // tb
    hq = PEER_HEADS
    row = functools.partial(_mod_row, rows_per_block=tb)
    route_shape = jax.ShapeDtypeStruct((hq, N_KEYS, n), jnp.float32)
    route_spec = pl.BlockSpec((1, N_KEYS, tb), lambda i, h: (h, 0, i))
    return pl.pallas_call(
        _peer_route_body,
        grid=(nb, hq),
        in_specs=[
            pl.BlockSpec((tb, d), lambda i, h: (i, 0)),
            pl.BlockSpec((1, d), lambda i, h: (0, 0)),
            pl.BlockSpec((1, 1, d), lambda i, h: (row(i), 0, 0)),
            pl.BlockSpec((1, 1, d), lambda i, h: (row(i), 0, 0)),
            pl.BlockSpec((d, D_QUERY), lambda i, h: (0, h)),
            pl.BlockSpec((1, N_KEYS, D_QUERY // 2), lambda i, h: (h, 0, 0)),
            pl.BlockSpec((1, N_KEYS, D_QUERY // 2), lambda i, h: (h, 0, 0)),
        ],
        out_specs=[pl.BlockSpec((tb, d), lambda i, h: (i, 0)), route_spec, route_spec, route_spec, route_spec],
        out_shape=[jax.ShapeDtypeStruct((n, d), jnp.bfloat16), route_shape, route_shape, route_shape, route_shape],
        scratch_shapes=[pltpu.VMEM((tb, d), jnp.bfloat16)],
        compiler_params=pltpu.CompilerParams(dimension_semantics=("parallel", "arbitrary"),
                                             vmem_limit_bytes=VMEM_LIMIT),
        name="peer_route",
    )(x, g, shift, scale, wq_b, k1_b, k2_b)


def _gelu(x):
    return 0.5 * x * (1.0 + lax.erf(x * np.float32(1.0 / np.sqrt(2.0))))


def _peer_expert_body(x_ref, gate_ref, hb_ref, th_ref, e1_ref, s2_ref, e2_ref, u_ref, vt_ref,
                      o_ref, acc_s, g_s, *, te, tb):
    j = pl.program_id(1)

    @pl.when(j == 0)
    def _():
        acc_s[...] = jnp.zeros_like(acc_s)

    nt = (((1,), (1,)), ((), ()))
    act = _gelu(lax.dot_general(u_ref[...], hb_ref[...], nt, preferred_element_type=jnp.float32))
    lanes = 128
    keys_per_tile = te // N_KEYS
    i0 = pl.multiple_of(j * keys_per_tile, keys_per_tile)
    for c in range(tb // lanes):
        cs = slice(c * lanes, (c + 1) * lanes)
        th = [th_ref[h, pl.ds(i0, keys_per_tile), cs] for h in range(PEER_HEADS)]
        e1 = [e1_ref[h, pl.ds(i0, keys_per_tile), cs] for h in range(PEER_HEADS)]
        for ii in range(keys_per_tile):
            w = jnp.zeros((N_KEYS, lanes), jnp.float32)
            for h in range(PEER_HEADS):
                keep = s2_ref[h, :, cs] >= th[h][ii:ii + 1]
                w = w + e1[h][ii:ii + 1] * jnp.where(keep, e2_ref[h, :, cs], 0.0)
            rs = slice(ii * N_KEYS, (ii + 1) * N_KEYS)
            g_s[rs, cs] = (w * act[rs, cs]).astype(jnp.bfloat16)
    acc_s[...] += jnp.dot(vt_ref[...], g_s[...], preferred_element_type=jnp.float32)

    @pl.when(j == pl.num_programs(1) - 1)
    def _():
        o_ref[...] = x_ref[...] + gate_ref[0] * acc_s[...].T


def peer_experts(x, gate, hb, th, e1, s2, e2, u_b, vt_b, tb, te):
    n, d = x.shape
    nb = n // tb
    ne = N_EXPERTS // te
    row = functools.partial(_mod_row, rows_per_block=tb)
    route_spec = pl.BlockSpec((PEER_HEADS, N_KEYS, tb), lambda i, j: (0, 0, i))
    return pl.pallas_call(
        functools.partial(_peer_expert_body, te=te, tb=tb),
        grid=(nb, ne),
        in_specs=[
            pl.BlockSpec((tb, d), lambda i, j: (i, 0)),
            pl.BlockSpec((1, 1, d), lambda i, j: (row(i), 0, 0)),
            pl.BlockSpec((tb, d), lambda i, j: (i, 0)),
            route_spec, route_spec, route_spec, route_spec,
            pl.BlockSpec((te, d), lambda i, j: (j, 0)),
            pl.BlockSpec((d, te), lambda i, j: (0, j)),
        ],
        out_specs=pl.BlockSpec((tb, d), lambda i, j: (i, 0)),
        out_shape=jax.ShapeDtypeStruct((n, d), jnp.float32),
        scratch_shapes=[pltpu.VMEM((d, tb), jnp.float32), pltpu.VMEM((te, tb), jnp.bfloat16)],
        compiler_params=pltpu.CompilerParams(dimension_semantics=("parallel", "arbitrary"),
                                             vmem_limit_bytes=VMEM_LIMIT),
        name="peer_experts",
    )(x, gate, hb, th, e1, s2, e2, u_b, vt_b)


def peer_layer(x, g, shift, scale, gate, w_q, keys, u_tab, v_tab, tb=512, te=1024):
    bf = jnp.bfloat16
    hb, th, e1, s2, e2 = peer_route(x, g, shift, scale, w_q.astype(bf), keys[0].astype(bf), keys[1].astype(bf), tb)
    return peer_experts(x, gate, hb, th, e1, s2, e2, u_tab.astype(bf), v_tab.T.astype(bf), tb, te)


def _final_rms_body(x_ref, g_ref, o_ref):
    x = x_ref[...]
    o_ref[...] = x * lax.rsqrt(jnp.mean(x * x, axis=-1, keepdims=True) + RMS_EPS) * g_ref[...]


def final_rmsnorm(x, g):
    n, d = x.shape
    rows = 512
    return pl.pallas_call(
        _final_rms_body,
        grid=(n // rows,),
        in_specs=[pl.BlockSpec((rows, d), lambda i: (i, 0)), pl.BlockSpec((1, d), lambda i: (0, 0))],
        out_specs=pl.BlockSpec((rows, d), lambda i: (i, 0)),
        out_shape=jax.ShapeDtypeStruct((n, d), x.dtype),
        compiler_params=pltpu.CompilerParams(dimension_semantics=("parallel",)),
        name="final_rmsnorm",
    )(x, g.reshape(1, d))


def kernel(x_prompt, x_sample, cache_k, cache_v, state_rwkv, c, c_ctx, mod_w, mod_b, norm_g,
           ab_w_in, ab_w_out, ab_sink, ab_shift_w, ab_k_k, ab_k_a, ab_r_k, ab_w0, ab_w2,
           ab_a0, ab_a2, ab_g2, ab_gn_g, ab_gn_b, cv_w_pw1, cv_b_pw1, cv_w_dw, cv_b_dw,
           cv_ln_g, cv_ln_b, cv_w_pw2, cv_b_pw2, peer_w_q, peer_keys, peer_u, peer_v, final_g):
    d = D_MODEL
    n_sample = DEC_BATCH * DEC_SEQ
    x = jnp.concatenate([x_prompt.reshape(N_PROMPT, d), x_sample.reshape(n_sample, d)], axis=0)
    cond = jnp.concatenate([c_ctx[None, :], c], axis=0)
    new_k, new_v, new_s = [], [], []
    for layer in range(DEPTH):
        i = layer // 2
        mod = fused_proj(cond, mod_w[layer], silu=True, bias=mod_b[layer], name="modulation")
        s1, c1, g1, s2, c2, g2 = (mod[:, j * d:(j + 1) * d].reshape(-1, 1, d) for j in range(6))
        norm_mod = (norm_g[layer, 0][None, :], s1, c1)
        if layer % 2 == 0:
            u_a, u_r = fused_proj(x, ab_w_in[i], norm_mod=norm_mod, splits=(D_ATTN_IN, D_RWKV_IN), name="in_proj")
            a_out = jnp.concatenate([
                context_attention_pallas(u_a, ab_sink[i]),
                window_attention_pallas(u_a, cache_k[:, i].reshape(DEC_BATCH, PAST_LEN, D_KV),
                                        cache_v[:, i].reshape(DEC_BATCH, PAST_LEN, D_KV), ab_sink[i])], axis=0)
            r, kh, v, w_f, ka_f, kt_f, w_b, ka_b, kt_b, g = rwkv_prep_pallas(
                u_r, ab_shift_w[i], ab_k_k[i], ab_k_a[i], ab_w0[i], ab_w2[i], ab_a0[i], ab_a2[i], ab_g2[i])
            scan_in = (r, kh, v, w_f, ka_f, kt_f, w_b, ka_b, kt_b)
            zeros = jnp.zeros((BATCH, 2, HEAD_PAIRS, RWKV_HEAD_DIM, LANES), jnp.float32)
            yf_p, yb_p, s_p = rwkv_scan_pallas(*scan_in, zeros, 0, BATCH, SEQ)
            yf_s, yb_s, _ = rwkv_scan_pallas(*scan_in, pack_state(state_rwkv[:, i]), N_PROMPT, DEC_BATCH, DEC_SEQ)
            y_f = jnp.concatenate([yf_p, yf_s], axis=0)
            y_b = jnp.concatenate([yb_p, yb_s], axis=0)
            x = rwkv_post_pallas(x, g1, a_out, y_f, y_b, r, v, kt_f, kt_b, g,
                                 ab_r_k[i], ab_gn_g[i], ab_gn_b[i], ab_w_out[i])
            new_k.append(u_a[:N_PROMPT, D_ATTN:D_ATTN + D_KV].reshape(BATCH, SEQ, KV_HEADS, HEAD_DIM))
            new_v.append(u_a[:N_PROMPT, D_ATTN + D_KV:].reshape(BATCH, SEQ, KV_HEADS, HEAD_DIM))
            new_s.append(unpack_state(s_p))
        else:
            u = fused_proj(x, cv_w_pw1[i], norm_mod=norm_mod, bias=cv_b_pw1[i], name="pw1")
            u = conformer_conv_pallas(u, cv_w_dw[i], cv_b_dw[i], cv_ln_g[i], cv_ln_b[i])
            x = fused_proj(u, cv_w_pw2[i], bias=cv_b_pw2[i], residual=(x, g1), name="pw2")
        x = peer_layer(x, norm_g[layer, 1][None, :], s2, c2, g2,
                       peer_w_q[layer], peer_keys[layer], peer_u[layer], peer_v[layer])
    y = final_rmsnorm(x, final_g)
    y_prompt = y[:N_PROMPT].reshape(BATCH, SEQ, d)
    y_sample = y[N_PROMPT:].reshape(DEC_BATCH, DEC_SEQ, d)
    return (y_prompt, y_sample, jnp.stack(new_k, axis=1), jnp.stack(new_v, axis=1), jnp.stack(new_s, axis=1))
```

```python
import functools
import jax
import jax.numpy as jnp
from jax import lax
import numpy as np
from jax.experimental import pallas as pl
from jax.experimental.pallas import tpu as pltpu

D_MODEL = 1024
BATCH = 32
SEQ = 256
DEPTH = 2
DEC_BATCH = 4
DEC_SEQ = 4096
PAST_LEN = 256

GRID_W = 64
BLOCK = 128
WINDOW = 128
ATTN_HEADS = 8
KV_HEADS = 2
GROUP = ATTN_HEADS // KV_HEADS
HEAD_DIM = 64
D_ATTN = ATTN_HEADS * HEAD_DIM
D_ATTN_IN = D_ATTN + 2 * KV_HEADS * HEAD_DIM
RWKV_HEADS = 8
RWKV_HEAD_DIM = 64
D_RWKV = RWKV_HEADS * RWKV_HEAD_DIM
LORA_W = 64
LORA_A = 64
LORA_G = 128
D_RWKV_IN = 3 * D_RWKV + 2 * LORA_W + 2 * LORA_A + LORA_G
D_IN_AB = D_ATTN_IN + D_RWKV_IN
SHIFT_WIDTH = 3
ROPE_BASE = 10000.0
D_CONV = D_MODEL
CONV_WIDTH = 31
PEER_HEADS = 8
N_KEYS = 128
N_EXPERTS = N_KEYS * N_KEYS
D_QUERY = 256
PEER_TOPK = 16
N_EVEN = (DEPTH + 1) // 2
N_ODD = DEPTH // 2
RMS_EPS = 1e-6
LN_EPS = 1e-5
GN_EPS = 64e-5
N_PROMPT = BATCH * SEQ
SAMPLE_LEN = DEC_SEQ
N_TOKENS = N_PROMPT + DEC_BATCH * DEC_SEQ
D_KV = KV_HEADS * HEAD_DIM
VMEM_LIMIT = 56 * 1024 * 1024
SUBLANES = 8
LANES = 128


def _mod_row(i, rows_per_block):
    tok = i * rows_per_block
    return jnp.where(tok < N_PROMPT, 0, 1 + (tok - N_PROMPT) // SAMPLE_LEN)


def _norm_mod(x, g, shift, scale):
    y = x * lax.rsqrt(jnp.mean(x * x, axis=-1, keepdims=True) + RMS_EPS)
    return (y * g) * (1 + scale) + shift


def _proj_body(*refs, prologue, has_bias, residual, splits):
    it = iter(refs)
    x_ref, w_ref = next(it), next(it)
    x = x_ref[...]
    if prologue == "norm_mod":
        g_ref, shift_ref, scale_ref = next(it), next(it), next(it)
        x = _norm_mod(x, g_ref[...], shift_ref[0], scale_ref[0])
    elif prologue == "silu":
        x = x * jax.nn.sigmoid(x)
    acc = jnp.dot(x.astype(jnp.bfloat16), w_ref[...], preferred_element_type=jnp.float32)
    if has_bias:
        acc = acc + next(it)[...]
    if residual:
        res_ref, gate_ref = next(it), next(it)
        acc = res_ref[...] + gate_ref[0] * acc
    start = 0
    for width in splits:
        next(it)[...] = acc[:, start:start + width]
        start += width


def fused_proj(x, w, *, norm_mod=None, silu=False, bias=None, residual=None, splits=None, tm=512, name="proj"):
    m, k = x.shape
    nout = w.shape[1]
    tm = min(tm, m)
    row = functools.partial(_mod_row, rows_per_block=tm)
    args = [x, w.astype(jnp.bfloat16)]
    specs = [pl.BlockSpec((tm, k), lambda i: (i, 0)), pl.BlockSpec((k, nout), lambda i: (0, 0))]
    prologue = None
    if norm_mod is not None:
        prologue = "norm_mod"
        args += list(norm_mod)
        specs += [pl.BlockSpec((1, k), lambda i: (0, 0)),
                  pl.BlockSpec((1, 1, k), lambda i: (row(i), 0, 0)),
                  pl.BlockSpec((1, 1, k), lambda i: (row(i), 0, 0))]
    elif silu:
        prologue = "silu"
    if bias is not None:
        args.append(bias.reshape(1, nout))
        specs.append(pl.BlockSpec((1, nout), lambda i: (0, 0)))
    if residual is not None:
        args += list(residual)
        specs += [pl.BlockSpec((tm, nout), lambda i: (i, 0)), pl.BlockSpec((1, 1, nout), lambda i: (row(i), 0, 0))]
    widths = (nout,) if splits is None else tuple(splits)
    assert sum(widths) == nout
    outs = pl.pallas_call(
        functools.partial(_proj_body, prologue=prologue, has_bias=bias is not None, residual=residual is not None,
                          splits=widths),
        grid=(m // tm,),
        in_specs=specs,
        out_specs=[pl.BlockSpec((tm, wd), lambda i: (i, 0)) for wd in widths],
        out_shape=[jax.ShapeDtypeStruct((m, wd), jnp.float32) for wd in widths],
        compiler_params=pltpu.CompilerParams(dimension_semantics=("parallel",), vmem_limit_bytes=VMEM_LIMIT),
        name=name,
    )(*args)
    return outs[0] if splits is None else outs


def _attend(q, keys, values, masks, sink):
    nt = (((1,), (1,)), ((), ()))
    scale = HEAD_DIM ** -0.5
    ss = []
    for k, msk in zip(keys, masks):
        s = lax.dot_general(q, k, nt, preferred_element_type=jnp.float32) * scale
        ss.append(s if msk is None else jnp.where(msk, s, -jnp.inf))
    m = sink
    for s in ss:
        m = jnp.maximum(m, jnp.max(s, axis=1, keepdims=True))
    ps = [jnp.exp(s - m) for s in ss]
    denom = jnp.exp(sink - m)
    for p in ps:
        denom = denom + jnp.sum(p, axis=1, keepdims=True)
    out = None
    for p, v in zip(ps, values):
        o = jnp.dot((p / denom).astype(jnp.bfloat16), v, preferred_element_type=jnp.float32)
        out = o if out is None else out + o
    return out


def _ctx_attn_body(q_ref, k_ref, v_ref, sink_ref, o_ref):
    bf = jnp.bfloat16
    q = q_ref[...].astype(bf)
    k = k_ref[...].astype(bf)
    v = v_ref[...].astype(bf)
    for h in range(ATTN_HEADS):
        kv = h // GROUP
        ks = slice(kv * HEAD_DIM, (kv + 1) * HEAD_DIM)
        hs = slice(h * HEAD_DIM, (h + 1) * HEAD_DIM)
        o_ref[:, hs] = _attend(q[:, hs], [k[:, ks]], [v[:, ks]], [None], sink_ref[0:1, h:h + 1])


def context_attention_pallas(u, sink):
    kcol = D_ATTN // D_KV
    return pl.pallas_call(
        _ctx_attn_body,
        grid=(BATCH,),
        in_specs=[pl.BlockSpec((SEQ, D_ATTN), lambda i: (i, 0)),
                  pl.BlockSpec((SEQ, D_KV), lambda i: (i, kcol)),
                  pl.BlockSpec((SEQ, D_KV), lambda i: (i, kcol + 1)),
                  pl.BlockSpec((1, ATTN_HEADS), lambda i: (0, 0))],
        out_specs=pl.BlockSpec((SEQ, D_ATTN), lambda i: (i, 0)),
        out_shape=jax.ShapeDtypeStruct((N_PROMPT, D_ATTN), jnp.float32),
        compiler_params=pltpu.CompilerParams(dimension_semantics=("parallel",), vmem_limit_bytes=VMEM_LIMIT),
        name="context_attention",
    )(u, u, u, sink.reshape(1, ATTN_HEADS))


def rope_tables():
    t = np.arange(DEC_SEQ)
    quarter = HEAD_DIM // 4
    freqs = jnp.asarray(ROPE_BASE, jnp.float32) ** (-jnp.arange(quarter, dtype=jnp.float32) / quarter)
    row = jnp.asarray(t // GRID_W, jnp.float32)[:, None] * freqs[None, :]
    col = jnp.asarray(t % GRID_W, jnp.float32)[:, None] * freqs[None, :]
    cos = jnp.concatenate([jnp.cos(row), jnp.cos(row), jnp.cos(col), jnp.cos(col)], axis=1)
    sin = jnp.concatenate([-jnp.sin(row), jnp.sin(row), -jnp.sin(col), jnp.sin(col)], axis=1)
    return cos, sin


def _rope(x, cos, sin, first):
    quarter = HEAD_DIM // 4
    width = x.shape[1]
    partner = jnp.where(first, pltpu.roll(x, width - quarter, 1), pltpu.roll(x, quarter, 1))
    return x * cos + partner * sin


def _win_attn_body(q_ref, kp_ref, kc_ref, kn_ref, vp_ref, vc_ref, vn_ref, kx_ref, vx_ref,
                   cq_ref, sq_ref, cp_ref, sp_ref, cn_ref, sn_ref, sink_ref, o_ref):
    bf = jnp.bfloat16
    n = pl.program_id(1)
    nb = pl.num_programs(1)
    quarter = HEAD_DIM // 4
    lane_q = lax.broadcasted_iota(jnp.int32, (BLOCK, D_ATTN), 1)
    lane_k = lax.broadcasted_iota(jnp.int32, (BLOCK, D_KV), 1)
    cq, sq = cq_ref[...], sq_ref[...]
    tile_q = lambda a: jnp.concatenate([a] * ATTN_HEADS, axis=1)
    tile_k = lambda a: jnp.concatenate([a] * KV_HEADS, axis=1)
    q = _rope(q_ref[...], tile_q(cq), tile_q(sq), (lane_q % (2 * quarter)) < quarter).astype(bf)
    first_k = (lane_k % (2 * quarter)) < quarter
    kp = _rope(kp_ref[...], tile_k(cp_ref[...]), tile_k(sp_ref[...]), first_k).astype(bf)
    kc = _rope(kc_ref[...], tile_k(cq), tile_k(sq), first_k).astype(bf)
    kn = _rope(kn_ref[...], tile_k(cn_ref[...]), tile_k(sn_ref[...]), first_k).astype(bf)
    vp, vc, vn = vp_ref[...].astype(bf), vc_ref[...].astype(bf), vn_ref[...].astype(bf)
    kx, vx = kx_ref[0].astype(bf), vx_ref[0].astype(bf)
    qi = lax.broadcasted_iota(jnp.int32, (BLOCK, BLOCK), 0)
    si = lax.broadcasted_iota(jnp.int32, (BLOCK, BLOCK), 1)
    m_prev = (si >= qi) & (n > 0)
    m_next = (si <= qi) & (n + 1 < nb)
    for h in range(ATTN_HEADS):
        kv = h // GROUP
        ks = slice(kv * HEAD_DIM, (kv + 1) * HEAD_DIM)
        hs = slice(h * HEAD_DIM, (h + 1) * HEAD_DIM)
        o_ref[:, hs] = _attend(q[:, hs], [kp[:, ks], kc[:, ks], kn[:, ks], kx[:, ks]],
                               [vp[:, ks], vc[:, ks], vn[:, ks], vx[:, ks]],
                               [m_prev, None, m_next, None], sink_ref[0:1, h:h + 1])


def window_attention_pallas(u, k_ctx, v_ctx, sink):
    nb = DEC_SEQ // BLOCK
    base = N_PROMPT // BLOCK
    kcol = D_ATTN // D_KV
    cos, sin = rope_tables()
    cur = lambda b, n: base + b * nb + n
    prev = lambda b, n: base + b * nb + jnp.maximum(n - 1, 0)
    nxt = lambda b, n: base + b * nb + jnp.minimum(n + 1, nb - 1)
    blk = lambda rowf, col, width: pl.BlockSpec((BLOCK, width), lambda b, n: (rowf(b, n), col))
    tab = lambda f: pl.BlockSpec((BLOCK, HEAD_DIM), lambda b, n: (f(n), 0))
    t_cur = lambda n: n
    t_prev = lambda n: jnp.maximum(n - 1, 0)
    t_next = lambda n: jnp.minimum(n + 1, nb - 1)
    ctx = pl.BlockSpec((1, PAST_LEN, D_KV), lambda b, n: (b, 0, 0))
    return pl.pallas_call(
        _win_attn_body,
        grid=(DEC_BATCH, nb),
        in_specs=[blk(cur, 0, D_ATTN),
                  blk(prev, kcol, D_KV), blk(cur, kcol, D_KV), blk(nxt, kcol, D_KV),
                  blk(prev, kcol + 1, D_KV), blk(cur, kcol + 1, D_KV), blk(nxt, kcol + 1, D_KV),
                  ctx, ctx,
                  tab(t_cur), tab(t_cur), tab(t_prev), tab(t_prev), tab(t_next), tab(t_next),
                  pl.BlockSpec((1, ATTN_HEADS), lambda b, n: (0, 0))],
        out_specs=pl.BlockSpec((BLOCK, D_ATTN), lambda b, n: (b * nb + n, 0)),
        out_shape=jax.ShapeDtypeStruct((DEC_BATCH * DEC_SEQ, D_ATTN), jnp.float32),
        compiler_params=pltpu.CompilerParams(dimension_semantics=("parallel", "parallel"),
                                             vmem_limit_bytes=VMEM_LIMIT),
        name="window_attention",
    )(u, u, u, u, u, u, u, k_ctx, v_ctx, cos, sin, cos, sin, cos, sin, sink.reshape(1, ATTN_HEADS))


CONV_ROWS = 256
CONV_HALO = 16


def _glu(u):
    return u[:, :D_CONV] * jax.nn.sigmoid(u[:, D_CONV:])


def _conv_body(up_ref, uc_ref, un_ref, w_ref, b_ref, g_ref, beta_ref, o_ref, ext):
    i = pl.program_id(0)
    blocks_per_sample = DEC_SEQ // CONV_ROWS
    j = (i - N_PROMPT // CONV_ROWS) % blocks_per_sample
    in_sample = i >= N_PROMPT // CONV_ROWS
    has_prev = in_sample & (j > 0)
    has_next = in_sample & (j < blocks_per_sample - 1)
    halo = CONV_HALO
    ext[0:halo] = jnp.where(has_prev, _glu(up_ref[CONV_ROWS - halo:, :]), 0.0)
    ext[halo:halo + CONV_ROWS] = _glu(uc_ref[...])
    ext[halo + CONV_ROWS:] = jnp.where(has_next, _glu(un_ref[0:halo, :]), 0.0)
    acc = jnp.zeros((CONV_ROWS, D_CONV), jnp.float32)
    for tap in range(CONV_WIDTH):
        start = halo - CONV_WIDTH // 2 + tap
        acc = acc + ext[start:start + CONV_ROWS, :] * w_ref[tap:tap + 1, :]
    y = acc + b_ref[...]
    mu = jnp.mean(y, axis=-1, keepdims=True)
    var = jnp.mean(jnp.square(y - mu), axis=-1, keepdims=True)
    y = (y - mu) * lax.rsqrt(var + LN_EPS) * g_ref[...] + beta_ref[...]
    o_ref[...] = y * jax.nn.sigmoid(y)


def conformer_conv_pallas(u, w_dw, b_dw, ln_g, ln_b):
    n = u.shape[0]
    nblk = n // CONV_ROWS
    vec = lambda a: a.reshape(1, D_CONV)
    row = lambda f: pl.BlockSpec((CONV_ROWS, 2 * D_CONV), lambda i: (f(i), 0))
    one = pl.BlockSpec((1, D_CONV), lambda i: (0, 0))
    return pl.pallas_call(
        _conv_body,
        grid=(nblk,),
        in_specs=[row(lambda i: jnp.maximum(i - 1, 0)), row(lambda i: i), row(lambda i: jnp.minimum(i + 1, nblk - 1)),
                  pl.BlockSpec((CONV_WIDTH, D_CONV), lambda i: (0, 0)), one, one, one],
        out_specs=pl.BlockSpec((CONV_ROWS, D_CONV), lambda i: (i, 0)),
        out_shape=jax.ShapeDtypeStruct((n, D_CONV), jnp.float32),
        scratch_shapes=[pltpu.VMEM((CONV_ROWS + 2 * CONV_HALO, D_CONV), jnp.float32)],
        compiler_params=pltpu.CompilerParams(dimension_semantics=("parallel",), vmem_limit_bytes=VMEM_LIMIT),
        name="conformer_conv",
    )(u, u, u, w_dw, vec(b_dw), vec(ln_g), vec(ln_b))


PREP_ROWS = 256
PREP_HALO = 8


def _softplus(z):
    return jnp.maximum(z, 0.0) + jnp.log1p(jnp.exp(-jnp.abs(z)))


def _rwkv_prep_body(up_ref, uc_ref, un_ref, sw_ref, kk_ref, ka_ref, w0_ref, w2_ref, a0_ref, a2_ref, g2_ref,
                    r_ref, kh_ref, v_ref, wf_ref, kaf_ref, ktf_ref, wb_ref, kab_ref, ktb_ref, g_ref, ext):
    i = pl.program_id(0)
    blocks_per_sample = DEC_SEQ // PREP_ROWS
    j = (i - N_PROMPT // PREP_ROWS) % blocks_per_sample
    in_sample = i >= N_PROMPT // PREP_ROWS
    has_prev = in_sample & (j > 0)
    has_next = in_sample & (j < blocks_per_sample - 1)
    halo = PREP_HALO
    ext[0:halo] = jnp.where(has_prev, up_ref[PREP_ROWS - halo:, :], 0.0)
    ext[halo:halo + PREP_ROWS] = uc_ref[...]
    ext[halo + PREP_ROWS:] = jnp.where(has_next, un_ref[0:halo, :], 0.0)
    u = jnp.zeros((PREP_ROWS, D_RWKV_IN), jnp.float32)
    for tap in range(SHIFT_WIDTH):
        start = halo - SHIFT_WIDTH // 2 + tap
        u = u + ext[start:start + PREP_ROWS, :] * sw_ref[tap:tap + 1, :]
    d = D_RWKV
    r, k, v = u[:, :d], u[:, d:2 * d], u[:, 2 * d:3 * d]
    o = 3 * d
    wl = (u[:, o:o + LORA_W], u[:, o + LORA_W:o + 2 * LORA_W])
    o += 2 * LORA_W
    al = (u[:, o:o + LORA_A], u[:, o + LORA_A:o + 2 * LORA_A])
    gl = u[:, o + 2 * LORA_A:]
    r_ref[...] = r
    v_ref[...] = v
    kk = k * kk_ref[...]
    n = RWKV_HEAD_DIM
    kh = jnp.concatenate(
        [kk[:, h * n:(h + 1) * n] / jnp.maximum(jnp.sqrt(jnp.sum(jnp.square(kk[:, h * n:(h + 1) * n]), axis=1, keepdims=True)), 1e-12)
         for h in range(RWKV_HEADS)], axis=1)
    kh_ref[...] = kh
    bf = jnp.bfloat16
    outs = ((wf_ref, kaf_ref, ktf_ref), (wb_ref, kab_ref, ktb_ref))
    for idx in range(2):
        lw = jnp.dot(jnp.tanh(wl[idx]).astype(bf), w2_ref[idx], preferred_element_type=jnp.float32)
        logw = -_softplus(-(w0_ref[idx:idx + 1, :] + lw)) - 0.5
        a = jax.nn.sigmoid(a0_ref[idx:idx + 1, :] + jnp.dot(al[idx].astype(bf), a2_ref[idx], preferred_element_type=jnp.float32))
        w_o, ka_o, kt_o = outs[idx]
        w_o[...] = jnp.exp(-jnp.exp(logw))
        ka_o[...] = kh * a
        kt_o[...] = k * (1 + (a - 1) * ka_ref[...])
    g_ref[...] = jnp.dot(jax.nn.sigmoid(gl).astype(bf), g2_ref[...], preferred_element_type=jnp.float32)


def rwkv_prep_pallas(u_r, shift_w, k_k, k_a, w0, w2, a0, a2, g2):
    n = u_r.shape[0]
    nblk = n // PREP_ROWS
    bf = jnp.bfloat16
    row = lambda f: pl.BlockSpec((PREP_ROWS, D_RWKV_IN), lambda i: (f(i), 0))
    full = lambda a: pl.BlockSpec(a.shape, lambda i: (0,) * a.ndim)
    params = [shift_w, k_k.reshape(1, D_RWKV), k_a.reshape(1, D_RWKV), w0, w2.astype(bf), a0, a2.astype(bf), g2.astype(bf)]
    out = jax.ShapeDtypeStruct((n, D_RWKV), jnp.float32)
    return pl.pallas_call(
        _rwkv_prep_body,
        grid=(nblk,),
        in_specs=[row(lambda i: jnp.maximum(i - 1, 0)), row(lambda i: i), row(lambda i: jnp.minimum(i + 1, nblk - 1))]
                 + [full(a) for a in params],
        out_specs=[pl.BlockSpec((PREP_ROWS, D_RWKV), lambda i: (i, 0))] * 10,
        out_shape=[out] * 10,
        scratch_shapes=[pltpu.VMEM((PREP_ROWS + 2 * PREP_HALO, D_RWKV_IN), jnp.float32)],
        compiler_params=pltpu.CompilerParams(dimension_semantics=("parallel",), vmem_limit_bytes=VMEM_LIMIT),
        name="rwkv_prep",
    )(u_r, u_r, u_r, *params)


def _rwkv_post_body(x_ref, gate_ref, a_ref, yf_ref, yb_ref, r_ref, v_ref, ktf_ref, ktb_ref, g_ref,
                    rk_ref, gng_ref, gnb_ref, wa_ref, wb_ref, o_ref):
    n = RWKV_HEAD_DIM
    y = yf_ref[...] + yb_ref[...]
    rkk = r_ref[...] * 0.5 * (ktf_ref[...] + ktb_ref[...]) * rk_ref[...]
    v = v_ref[...]
    parts = []
    for h in range(RWKV_HEADS):
        hs = slice(h * n, (h + 1) * n)
        yh = y[:, hs]
        mu = jnp.mean(yh, axis=1, keepdims=True)
        var = jnp.mean(jnp.square(yh - mu), axis=1, keepdims=True)
        yn = (yh - mu) * lax.rsqrt(var + GN_EPS) * gng_ref[:, hs] + gnb_ref[:, hs]
        parts.append(yn + jnp.sum(rkk[:, hs], axis=1, keepdims=True) * v[:, hs])
    b_out = jnp.concatenate(parts, axis=1) * g_ref[...]
    bf = jnp.bfloat16
    acc = jnp.dot(a_ref[...].astype(bf), wa_ref[...], preferred_element_type=jnp.float32)
    acc = acc + jnp.dot(b_out.astype(bf), wb_ref[...], preferred_element_type=jnp.float32)
    o_ref[...] = x_ref[...] + gate_ref[0] * acc


def rwkv_post_pallas(x, gate, a_out, y_f, y_b, r, v, kt_f, kt_b, g, r_k, gn_g, gn_b, w_out, tm=512):
    n, d = x.shape
    bf = jnp.bfloat16
    rowmap = functools.partial(_mod_row, rows_per_block=tm)
    tok = lambda width: pl.BlockSpec((tm, width), lambda i: (i, 0))
    vec = pl.BlockSpec((1, D_RWKV), lambda i: (0, 0))
    wsp = pl.BlockSpec((D_RWKV, d), lambda i: (0, 0))
    return pl.pallas_call(
        _rwkv_post_body,
        grid=(n // tm,),
        in_specs=[tok(d), pl.BlockSpec((1, 1, d), lambda i: (rowmap(i), 0, 0))] + [tok(D_RWKV)] * 8 + [vec] * 3 + [wsp] * 2,
        out_specs=tok(d),
        out_shape=jax.ShapeDtypeStruct((n, d), jnp.float32),
        compiler_params=pltpu.CompilerParams(dimension_semantics=("parallel",), vmem_limit_bytes=VMEM_LIMIT),
        name="rwkv_post",
    )(x, gate, a_out, y_f, y_b, r, v, kt_f, kt_b, g, r_k.reshape(1, D_RWKV), gn_g.reshape(1, D_RWKV),
      gn_b.reshape(1, D_RWKV), w_out[:D_ATTN].astype(bf), w_out[D_ATTN:].astype(bf))


HEAD_PAIRS = RWKV_HEADS // 2
SCAN_STEPS = 128


def _seg_sums(x, seg_a):
    return (jnp.sum(jnp.where(seg_a, x, 0.0), axis=1, keepdims=True),
            jnp.sum(jnp.where(seg_a, 0.0, x), axis=1, keepdims=True))


def _rwkv_scan_body(r_f, kh_f, v_f, w_f, ka_f, kt_f, r_b, kh_b, v_b, w_b, ka_b, kt_b, s0_ref,
                    yf_ref, yb_ref, sfin_ref, s_scr, yacc, vt_scr):
    tb = pl.program_id(1)
    n = RWKV_HEAD_DIM
    tt = SCAN_STEPS

    @pl.when(tb == 0)
    def _():
        s_scr[...] = s0_ref[0]

    ins = ((r_f, kh_f, v_f, w_f, ka_f, kt_f), (r_b, kh_b, v_b, w_b, ka_b, kt_b))
    bf = jnp.bfloat16
    for d in range(2):
        for p in range(HEAD_PAIRS):
            vt = ins[d][2][:, p * LANES:(p + 1) * LANES].T
            hi = vt.astype(bf)
            rest = vt - hi.astype(jnp.float32)
            mid = rest.astype(bf)
            lo = (rest - mid.astype(jnp.float32)).astype(bf)
            for piece, val in enumerate((hi, mid, lo)):
                vt_scr[d, p * LANES:(p + 1) * LANES, piece * tt:(piece + 1) * tt] = val
    yacc[...] = jnp.zeros_like(yacc)

    lane = lax.broadcasted_iota(jnp.int32, (n, LANES), 1)
    seg_a = lane < n
    lane2 = lax.broadcasted_iota(jnp.int32, (LANES, LANES), 1)
    sel_row = lax.broadcasted_iota(jnp.int32, (3 * tt, LANES), 0) % tt
    groups = tt // SUBLANES

    def group(g, carry):
        bases = (pl.multiple_of(g * SUBLANES, SUBLANES), pl.multiple_of((groups - 1 - g) * SUBLANES, SUBLANES))
        rows = [[ref[pl.ds(bases[d], SUBLANES), :] for ref in (ins[d][0], ins[d][1], ins[d][3], ins[d][4], ins[d][5])]
                for d in range(2)]
        pairs = range(HEAD_PAIRS)
        for step in range(SUBLANES):
            for d in range(2):
                ii = step if d == 0 else SUBLANES - 1 - step
                at_t = lane2 == bases[d] + ii
                vec = [[x[ii:ii + 1, p * LANES:(p + 1) * LANES] for x in rows[d]] for p in pairs]
                s_old = [s_scr[d, p] for p in pairs]
                sk = [_seg_sums(s_old[p] * vec[p][1], seg_a) for p in pairs]
                sel = jnp.where(sel_row == bases[d] + ii, 1.0, 0.0).astype(bf)
                vcol = jnp.dot(vt_scr[d], sel, preferred_element_type=jnp.float32)
                ysum = []
                for p in pairs:
                    r_t, _, w_t, ka_t, kt_t = vec[p]
                    vc = jnp.where(seg_a, vcol[p * LANES:p * LANES + n], vcol[p * LANES + n:(p + 1) * LANES])
                    s = s_old[p] * w_t - jnp.where(seg_a, *sk[p]) * ka_t + vc * kt_t
                    s_scr[d, p] = s
                    ysum.append(_seg_sums(s * r_t, seg_a))
                for p in pairs:
                    ya, yb = ysum[p]
                    y = jnp.concatenate([jnp.broadcast_to(ya, (n, LANES)), jnp.broadcast_to(yb, (n, LANES))], axis=0)
                    yacc[d, p] = jnp.where(at_t, y, yacc[d, p])
        return carry

    lax.fori_loop(0, groups, group, 0)
    for p in range(HEAD_PAIRS):
        yf_ref[:, p * LANES:(p + 1) * LANES] = yacc[0, p].T
        yb_ref[:, p * LANES:(p + 1) * LANES] = yacc[1, p].T

    @pl.when(tb == pl.num_programs(1) - 1)
    def _():
        sfin_ref[0] = s_scr[...]


def rwkv_scan_pallas(r, kh, v, w_f, ka_f, kt_f, w_b, ka_b, kt_b, s0p, row0, b, t):
    dm = r.shape[1]
    tt = SCAN_STEPS
    nt = t // tt
    blk0 = row0 // tt
    fwd_in = pl.BlockSpec((tt, dm), lambda i, j: (blk0 + i * nt + j, 0))
    bwd_in = pl.BlockSpec((tt, dm), lambda i, j: (blk0 + i * nt + nt - 1 - j, 0))
    fwd_out = pl.BlockSpec((tt, dm), lambda i, j: (i * nt + j, 0))
    bwd_out = pl.BlockSpec((tt, dm), lambda i, j: (i * nt + nt - 1 - j, 0))
    st = pl.BlockSpec((1, 2, HEAD_PAIRS, RWKV_HEAD_DIM, LANES), lambda i, j: (i, 0, 0, 0, 0))
    return pl.pallas_call(
        _rwkv_scan_body,
        grid=(b, nt),
        in_specs=[fwd_in] * 6 + [bwd_in] * 6 + [st],
        out_specs=[fwd_out, bwd_out, st],
        out_shape=[jax.ShapeDtypeStruct((b * t, dm), jnp.float32), jax.ShapeDtypeStruct((b * t, dm), jnp.float32),
                   jax.ShapeDtypeStruct(s0p.shape, jnp.float32)],
        scratch_shapes=[pltpu.VMEM((2, HEAD_PAIRS, RWKV_HEAD_DIM, LANES), jnp.float32),
                        pltpu.VMEM((2, HEAD_PAIRS, LANES, LANES), jnp.float32),
                        pltpu.VMEM((2, HEAD_PAIRS * LANES, 3 * tt), jnp.bfloat16)],
        compiler_params=pltpu.CompilerParams(dimension_semantics=("parallel", "arbitrary"),
                                             vmem_limit_bytes=VMEM_LIMIT),
        name="rwkv_scan",
    )(r, kh, v, w_f, ka_f, kt_f, r, kh, v, w_b, ka_b, kt_b, s0p)


def pack_state(s):
    b = s.shape[0]
    s = s.reshape(b, 2, HEAD_PAIRS, 2, RWKV_HEAD_DIM, RWKV_HEAD_DIM)
    return s.transpose(0, 1, 2, 4, 3, 5).reshape(b, 2, HEAD_PAIRS, RWKV_HEAD_DIM, LANES)


def unpack_state(sp):
    b = sp.shape[0]
    s = sp.reshape(b, 2, HEAD_PAIRS, RWKV_HEAD_DIM, 2, RWKV_HEAD_DIM)
    return s.transpose(0, 1, 2, 4, 3, 5).reshape(b, 2, RWKV_HEADS, RWKV_HEAD_DIM, RWKV_HEAD_DIM)


def _top_values(s, n):
    vals = []
    cur = s
    for a in range(n):
        m = jnp.max(cur, axis=0, keepdims=True)
        vals.append(m)
        if a + 1 < n:
            cur = jnp.where(cur == m, -jnp.inf, cur)
    return vals


def _peer_route_body(x_ref, g_ref, shift_ref, scale_ref, wq_ref, k1_ref, k2_ref,
                     hb_ref, th_ref, e1_ref, s2_ref, e2_ref, hb_s):
    h = pl.program_id(1)

    @pl.when(h == 0)
    def _():
        hn = _norm_mod(x_ref[...], g_ref[...], shift_ref[0], scale_ref[0])
        hb_s[...] = hn.astype(jnp.bfloat16)
        hb_ref[...] = hn.astype(jnp.bfloat16)

    half = D_QUERY // 2
    q = jnp.dot(hb_s[...], wq_ref[...], preferred_element_type=jnp.float32).astype(jnp.bfloat16)
    nt = (((1,), (1,)), ((), ()))
    s1 = lax.dot_general(k1_ref[0], q[:, :half], nt, preferred_element_type=jnp.float32)
    s2 = lax.dot_general(k2_ref[0], q[:, half:], nt, preferred_element_type=jnp.float32)
    k = PEER_TOPK
    v1 = _top_values(s1, k)
    v2 = _top_values(s2, k)
    v2s = jnp.concatenate(v2, axis=0)
    cand = [v1[0] + v2s]
    cand += [v1[a] + v2s[:8] for a in range(1, 8)]
    cand += [jnp.concatenate(v1[8:], axis=0) + v2[0]]
    tau = _top_values(jnp.concatenate(cand, axis=0), k)[-1]
    e2v = jnp.exp(v2s - v2[0])
    z = jnp.zeros_like(tau)
    theta = []
    for a in range(k):
        sel = (v1[a] + v2s) >= tau
        theta.append(jnp.min(jnp.where(sel, v2s, jnp.inf), axis=0, keepdims=True))
        z = z + jnp.exp(v1[a] - v1[0]) * jnp.sum(jnp.where(sel, e2v, 0.0), axis=0, keepdims=True)
    th = jnp.full_like(s1, jnp.inf)
    for a in range(k):
        th = jnp.where(s1 == v1[a], theta[a], th)
    th_ref[0] = th
    e1_ref[0] = jnp.where(s1 >= v1[k - 1], jnp.exp(s1 - v1[0]) / z, 0.0)
    s2_ref[0] = s2
    e2_ref[0] = jnp.exp(s2 - v2[0])


def peer_route(x, g, shift, scale, wq_b, k1_b, k2_b, tb):
    n, d = x.shape
    nb = n // tb
    hq = PEER_HEADS
    row = functools.partial(_mod_row, rows_per_block=tb)
    route_shape = jax.ShapeDtypeStruct((hq, N_KEYS, n), jnp.float32)
    route_spec = pl.BlockSpec((1, N_KEYS, tb), lambda i, h: (h, 0, i))
    return pl.pallas_call(
        _peer_route_body,
        grid=(nb, hq),
        in_specs=[
            pl.BlockSpec((tb, d), lambda i, h: (i, 0)),
            pl.BlockSpec((1, d), lambda i, h: (0, 0)),
            pl.BlockSpec((1, 1, d), lambda i, h: (row(i), 0, 0)),
            pl.BlockSpec((1, 1, d), lambda i, h: (row(i), 0, 0)),
            pl.BlockSpec((d, D_QUERY), lambda i, h: (0, h)),
            pl.BlockSpec((1, N_KEYS, D_QUERY // 2), lambda i, h: (h, 0, 0)),
            pl.BlockSpec((1, N_KEYS, D_QUERY // 2), lambda i, h: (h, 0, 0)),
        ],
        out_specs=[pl.BlockSpec((tb, d), lambda i, h: (i, 0)), route_spec, route_spec, route_spec, route_spec],
        out_shape=[jax.ShapeDtypeStruct((n, d), jnp.bfloat16), route_shape, route_shape, route_shape, route_shape],
        scratch_shapes=[pltpu.VMEM((tb, d), jnp.bfloat16)],
        compiler_params=pltpu.CompilerParams(dimension_semantics=("parallel", "arbitrary"),
                                             vmem_limit_bytes=VMEM_LIMIT),
        name="peer_route",
    )(x, g, shift, scale, wq_b, k1_b, k2_b)


def _gelu(x):
    return 0.5 * x * (1.0 + lax.erf(x * np.float32(1.0 / np.sqrt(2.0))))


def _peer_expert_body(x_ref, gate_ref, hb_ref, th_ref, e1_ref, s2_ref, e2_ref, u_ref, vt_ref,
                      o_ref, acc_s, g_s, *, te, tb):
    j = pl.program_id(1)

    @pl.when(j == 0)
    def _():
        acc_s[...] = jnp.zeros_like(acc_s)

    nt = (((1,), (1,)), ((), ()))
    act = _gelu(lax.dot_general(u_ref[...], hb_ref[...], nt, preferred_element_type=jnp.float32))
    lanes = 128
    keys_per_tile = te // N_KEYS
    i0 = pl.multiple_of(j * keys_per_tile, keys_per_tile)
    for c in range(tb // lanes):
        cs = slice(c * lanes, (c + 1) * lanes)
        th = [th_ref[h, pl.ds(i0, keys_per_tile), cs] for h in range(PEER_HEADS)]
        e1 = [e1_ref[h, pl.ds(i0, keys_per_tile), cs] for h in range(PEER_HEADS)]
        for ii in range(keys_per_tile):
            w = jnp.zeros((N_KEYS, lanes), jnp.float32)
            for h in range(PEER_HEADS):
                keep = s2_ref[h, :, cs] >= th[h][ii:ii + 1]
                w = w + e1[h][ii:ii + 1] * jnp.where(keep, e2_ref[h, :, cs], 0.0)
            rs = slice(ii * N_KEYS, (ii + 1) * N_KEYS)
            g_s[rs, cs] = (w * act[rs, cs]).astype(jnp.bfloat16)
    acc_s[...] += jnp.dot(vt_ref[...], g_s[...], preferred_element_type=jnp.float32)

    @pl.when(j == pl.num_programs(1) - 1)
    def _():
        o_ref[...] = x_ref[...] + gate_ref[0] * acc_s[...].T


def peer_experts(x, gate, hb, th, e1, s2, e2, u_b, vt_b, tb, te):
    n, d = x.shape
    nb = n // tb
    ne = N_EXPERTS // te
    row = functools.partial(_mod_row, rows_per_block=tb)
    route_spec = pl.BlockSpec((PEER_HEADS, N_KEYS, tb), lambda i, j: (0, 0, i))
    return pl.pallas_call(
        functools.partial(_peer_expert_body, te=te, tb=tb),
        grid=(nb, ne),
        in_specs=[
            pl.BlockSpec((tb, d), lambda i, j: (i, 0)),
            pl.BlockSpec((1, 1, d), lambda i, j: (row(i), 0, 0)),
            pl.BlockSpec((tb, d), lambda i, j: (i, 0)),
            route_spec, route_spec, route_spec, route_spec,
            pl.BlockSpec((te, d), lambda i, j: (j, 0)),
            pl.BlockSpec((d, te), lambda i, j: (0, j)),
        ],
        out_specs=pl.BlockSpec((tb, d), lambda i, j: (i, 0)),
        out_shape=jax.ShapeDtypeStruct((n, d), jnp.float32),
        scratch_shapes=[pltpu.VMEM((d, tb), jnp.float32), pltpu.VMEM((te, tb), jnp.bfloat16)],
        compiler_params=pltpu.CompilerParams(dimension_semantics=("parallel", "arbitrary"),
                                             vmem_limit_bytes=VMEM_LIMIT),
        name="peer_experts",
    )(x, gate, hb, th, e1, s2, e2, u_b, vt_b)


def peer_layer(x, g, shift, scale, gate, w_q, keys, u_tab, v_tab, tb=512, te=2048):
    bf = jnp.bfloat16
    hb, th, e1, s2, e2 = peer_route(x, g, shift, scale, w_q.astype(bf), keys[0].astype(bf), keys[1].astype(bf), tb)
    return peer_experts(x, gate, hb, th, e1, s2, e2, u_tab.astype(bf), v_tab.T.astype(bf), tb, te)


def _final_rms_body(x_ref, g_ref, o_ref):
    x = x_ref[...]
    o_ref[...] = x * lax.rsqrt(jnp.mean(x * x, axis=-1, keepdims=True) + RMS_EPS) * g_ref[...]


def final_rmsnorm(x, g):
    n, d = x.shape
    rows = 512
    return pl.pallas_call(
        _final_rms_body,
        grid=(n // rows,),
        in_specs=[pl.BlockSpec((rows, d), lambda i: (i, 0)), pl.BlockSpec((1, d), lambda i: (0, 0))],
        out_specs=pl.BlockSpec((rows, d), lambda i: (i, 0)),
        out_shape=jax.ShapeDtypeStruct((n, d), x.dtype),
        compiler_params=pltpu.CompilerParams(dimension_semantics=("parallel",)),
        name="final_rmsnorm",
    )(x, g.reshape(1, d))


def kernel(x_prompt, x_sample, cache_k, cache_v, state_rwkv, c, c_ctx, mod_w, mod_b, norm_g,
           ab_w_in, ab_w_out, ab_sink, ab_shift_w, ab_k_k, ab_k_a, ab_r_k, ab_w0, ab_w2,
           ab_a0, ab_a2, ab_g2, ab_gn_g, ab_gn_b, cv_w_pw1, cv_b_pw1, cv_w_dw, cv_b_dw,
           cv_ln_g, cv_ln_b, cv_w_pw2, cv_b_pw2, peer_w_q, peer_keys, peer_u, peer_v, final_g):
    d = D_MODEL
    n_sample = DEC_BATCH * DEC_SEQ
    x = jnp.concatenate([x_prompt.reshape(N_PROMPT, d), x_sample.reshape(n_sample, d)], axis=0)
    cond = jnp.concatenate([c_ctx[None, :], c], axis=0)
    new_k, new_v, new_s = [], [], []
    for layer in range(DEPTH):
        i = layer // 2
        mod = fused_proj(cond, mod_w[layer], silu=True, bias=mod_b[layer], name="modulation")
        s1, c1, g1, s2, c2, g2 = (mod[:, j * d:(j + 1) * d].reshape(-1, 1, d) for j in range(6))
        norm_mod = (norm_g[layer, 0][None, :], s1, c1)
        if layer % 2 == 0:
            u_a, u_r = fused_proj(x, ab_w_in[i], norm_mod=norm_mod, splits=(D_ATTN_IN, D_RWKV_IN), name="in_proj")
            a_out = jnp.concatenate([
                context_attention_pallas(u_a, ab_sink[i]),
                window_attention_pallas(u_a, cache_k[:, i].reshape(DEC_BATCH, PAST_LEN, D_KV),
                                        cache_v[:, i].reshape(DEC_BATCH, PAST_LEN, D_KV), ab_sink[i])], axis=0)
            r, kh, v, w_f, ka_f, kt_f, w_b, ka_b, kt_b, g = rwkv_prep_pallas(
                u_r, ab_shift_w[i], ab_k_k[i], ab_k_a[i], ab_w0[i], ab_w2[i], ab_a0[i], ab_a2[i], ab_g2[i])
            scan_in = (r, kh, v, w_f, ka_f, kt_f, w_b, ka_b, kt_b)
            zeros = jnp.zeros((BATCH, 2, HEAD_PAIRS, RWKV_HEAD_DIM, LANES), jnp.float32)
            yf_p, yb_p, s_p = rwkv_scan_pallas(*scan_in, zeros, 0, BATCH, SEQ)
            yf_s, yb_s, _ = rwkv_scan_pallas(*scan_in, pack_state(state_rwkv[:, i]), N_PROMPT, DEC_BATCH, DEC_SEQ)
            y_f = jnp.concatenate([yf_p, yf_s], axis=0)
            y_b = jnp.concatenate([yb_p, yb_s], axis=0)
            x = rwkv_post_pallas(x, g1, a_out, y_f, y_b, r, v, kt_f, kt_b, g,
                                 ab_r_k[i], ab_gn_g[i], ab_gn_b[i], ab_w_out[i])
            new_k.append(u_a[:N_PROMPT, D_ATTN:D_ATTN + D_KV].reshape(BATCH, SEQ, KV_HEADS, HEAD_DIM))
            new_v.append(u_a[:N_PROMPT, D_ATTN + D_KV:].reshape(BATCH, SEQ, KV_HEADS, HEAD_DIM))
            new_s.append(unpack_state(s_p))
        else:
            u = fused_proj(x, cv_w_pw1[i], norm_mod=norm_mod, bias=cv_b_pw1[i], name="pw1")
            u = conformer_conv_pallas(u, cv_w_dw[i], cv_b_dw[i], cv_ln_g[i], cv_ln_b[i])
            x = fused_proj(u, cv_w_pw2[i], bias=cv_b_pw2[i], residual=(x, g1), name="pw2")
        x = peer_layer(x, norm_g[layer, 1][None, :], s2, c2, g2,
                       peer_w_q[layer], peer_keys[layer], peer_u[layer], peer_v[layer])
    y = final_rmsnorm(x, final_g)
    y_prompt = y[:N_PROMPT].reshape(BATCH, SEQ, d)
    y_sample = y[N_PROMPT:].reshape(DEC_BATCH, DEC_SEQ, d)
    return (y_prompt, y_sample, jnp.stack(new_k, axis=1), jnp.stack(new_v, axis=1), jnp.stack(new_s, axis=1))
```
